```python
import math, functools
import jax, jax.numpy as jnp
from jax import lax
import numpy as np

D_MODEL = 1024
BATCH = 2
SEQ = 16384
DEPTH = 4
DEC_BATCH = 32
DEC_SEQ = 64
PAST_LEN = 1024

CHUNK = 64
Q_BLOCK = 128
D_FF = 2816
CONV_DIM = 256
SHORT_CONV_W = 3
SB_HEADS = 4
HEAD_DIM = 64
SB_DIM = SB_HEADS * HEAD_DIM
SSD_HEADS = 8
SSD_HEAD_DIM = 64
SSD_INNER = SSD_HEADS * SSD_HEAD_DIM
SSD_GROUPS = 2
SSD_STATE = 128
SSD_CONV_W = 4
SSD_CONV_DIM = SSD_INNER + 2 * SSD_GROUPS * SSD_STATE
D_MIX = CONV_DIM + SB_DIM + SSD_INNER
D_IN_PROJ = 3 * CONV_DIM + 3 * SB_DIM + SSD_INNER + SSD_CONV_DIM + SSD_HEADS
ALPHA = (2 * DEPTH) ** 0.25
BETA_INIT = (8 * DEPTH) ** -0.25
LN_EPS = 1e-5
RMS_EPS = 1e-5

kernel_name = 'hybrid_streaming_encoder_step'


def layer_norm(x, g, b):
    xf = x.astype(jnp.float32)
    mu = jnp.mean(xf, axis=-1, keepdims=True)
    var = jnp.mean(jnp.square(xf - mu), axis=-1, keepdims=True)
    return ((xf - mu) * lax.rsqrt(var + LN_EPS) * g + b).astype(x.dtype)


def rms_norm(xf, g):
    return xf * lax.rsqrt(jnp.mean(jnp.square(xf), axis=-1, keepdims=True) + RMS_EPS) * g


def swiglu(x, w1, w3, w2):
    return (jax.nn.silu(x @ w1) * (x @ w3)) @ w2


def causal_conv(u, prev, w):
    width = w.shape[0]
    length = u.shape[1]
    up = jnp.concatenate([prev.astype(u.dtype), u], axis=1)
    y = sum(up[:, k:k + length, :] * w[k] for k in range(width))
    return y, up[:, length:, :]


def stick_breaking(q, k, v, q_pos, k_pos):
    z = jnp.einsum('bqhd,bkhd->bhqk', q.astype(jnp.float32), k.astype(jnp.float32)) * (HEAD_DIM ** -0.5)
    mask = k_pos[None, :] < q_pos[:, None]
    log_1m = jnp.where(mask, jax.nn.log_sigmoid(-z), 0.0)
    a = jnp.where(mask, jnp.exp(z + lax.cumsum(log_1m, axis=3, reverse=True)), 0.0)
    return jnp.einsum('bhqk,bkhd->bqhd', a, v.astype(jnp.float32)).astype(v.dtype)


def stick_breaking_prompt(q, k, v):
    length = q.shape[1]
    outs = []
    for i in range(length // Q_BLOCK):
        start, end = i * Q_BLOCK, (i + 1) * Q_BLOCK
        outs.append(stick_breaking(q[:, start:end], k[:, :end], v[:, :end],
                                   start + jnp.arange(Q_BLOCK), jnp.arange(end)))
    return jnp.concatenate(outs, axis=1)


def stick_breaking_sample(k_past, v_past, q, k, v):
    past = k_past.shape[1]
    length = q.shape[1]
    k_all = jnp.concatenate([k_past.astype(k.dtype), k], axis=1)
    v_all = jnp.concatenate([v_past.astype(v.dtype), v], axis=1)
    return stick_breaking(q, k_all, v_all, past + jnp.arange(length), jnp.arange(past + length))


def ssd_scan(x, dt, a, b_mat, c_mat, h0, chunk):
    f32 = jnp.float32
    bsz, length, n_heads, p_dim = x.shape
    n_chunks = length // chunk
    rep = n_heads // b_mat.shape[2]
    xc = x.astype(f32).reshape(bsz, n_chunks, chunk, n_heads, p_dim)
    bc = jnp.repeat(b_mat.astype(f32), rep, axis=2).reshape(bsz, n_chunks, chunk, n_heads, -1)
    cc = jnp.repeat(c_mat.astype(f32), rep, axis=2).reshape(bsz, n_chunks, chunk, n_heads, -1)
    dtc = dt.reshape(bsz, n_chunks, chunk, n_heads)
    a_cum = jnp.cumsum(dtc * a, axis=2)
    causal = jnp.tril(jnp.ones((chunk, chunk), dtype=bool))
    seg = a_cum[:, :, :, None, :] - a_cum[:, :, None, :, :]
    decay = jnp.exp(jnp.where(causal[None, None, :, :, None], seg, -jnp.inf))
    scores = jnp.einsum('bcihn,bcjhn->bcijh', cc, bc) * decay
    y_diag = jnp.einsum('bcijh,bcjh,bcjhp->bcihp', scores, dtc, xc)
    decay_end = jnp.exp(a_cum[:, :, -1:, :] - a_cum)
    states = jnp.einsum('bcjh,bcjhn,bcjhp->bchpn', decay_end * dtc, bc, xc)
    chunk_decay = jnp.exp(a_cum[:, :, -1, :])

    def step(h, inp):
        st, dec = inp
        return dec[:, :, None, None] * h + st, h

    h_final, h_start = lax.scan(step, h0.astype(f32),
                                (jnp.moveaxis(states, 1, 0), jnp.moveaxis(chunk_decay, 1, 0)))
    h_start = jnp.moveaxis(h_start, 0, 1)
    y_off = jnp.einsum('bcihn,bchpn,bcih->bcihp', cc, h_start, jnp.exp(a_cum))
    return (y_diag + y_off).reshape(bsz, length, n_heads, p_dim), h_final


def token_mixers(h, p, l, conv_prev, ssd_conv_prev, ssd_h0, ssd_chunk, attend):
    f32 = jnp.float32
    bsz, length, _ = h.shape
    sizes = [CONV_DIM] * 3 + [SB_DIM] * 3 + [SSD_INNER, SSD_CONV_DIM, SSD_HEADS]
    idx = [int(i) for i in np.cumsum(sizes)[:-1]]
    proj = h @ p['w_in'][l]
    b_gate, c_gate, u_in, q, k, v, z, xbc, dt_raw = jnp.split(proj, idx, axis=-1)
    conv_out, conv_state = causal_conv(c_gate * u_in, conv_prev, p['short_conv_w'][l])
    y_conv = b_gate * conv_out
    q = q.reshape(bsz, length, SB_HEADS, HEAD_DIM)
    k = k.reshape(bsz, length, SB_HEADS, HEAD_DIM)
    v = v.reshape(bsz, length, SB_HEADS, HEAD_DIM)
    y_sb = attend(q, k, v).reshape(bsz, length, SB_DIM)
    xbc_c, ssd_conv_state = causal_conv(xbc, ssd_conv_prev, p['ssd_conv_w'][l])
    xbc_c = jax.nn.silu(xbc_c + p['ssd_conv_b'][l])
    xs, b_mat, c_mat = jnp.split(xbc_c, [SSD_INNER, SSD_INNER + SSD_GROUPS * SSD_STATE], axis=-1)
    xs = xs.reshape(bsz, length, SSD_HEADS, SSD_HEAD_DIM)
    b_mat = b_mat.reshape(bsz, length, SSD_GROUPS, SSD_STATE)
    c_mat = c_mat.reshape(bsz, length, SSD_GROUPS, SSD_STATE)
    dt = jax.nn.softplus(dt_raw.astype(f32) + p['ssd_dt_bias'][l].astype(f32))
    a = -jnp.exp(p['ssd_a_log'][l].astype(f32))
    y_ssm, ssd_state = ssd_scan(xs, dt, a, b_mat, c_mat, ssd_h0, ssd_chunk)
    y_ssm = y_ssm + p['ssd_d'][l].astype(f32)[:, None] * xs.astype(f32)
    y_ssm = y_ssm.reshape(bsz, length, SSD_INNER) * jax.nn.silu(z.astype(f32))
    y_ssd = rms_norm(y_ssm, p['ssd_norm_g'][l].astype(f32)).astype(h.dtype)
    mix = jnp.concatenate([y_conv, y_sb, y_ssd], axis=-1) @ p['w_out'][l]
    return mix, (k, v, conv_state, ssd_conv_state, ssd_state.astype(ssd_h0.dtype))


def trunk_layer(x, p, l, conv_prev, ssd_conv_prev, ssd_h0, ssd_chunk, attend):
    x = layer_norm(ALPHA * x + 0.5 * swiglu(x, p['ffn1_w1'][l], p['ffn1_w3'][l], p['ffn1_w2'][l]),
                   p['ln1_g'][l], p['ln1_b'][l])
    mix, st = token_mixers(x, p, l, conv_prev, ssd_conv_prev, ssd_h0, ssd_chunk, attend)
    x = layer_norm(ALPHA * x + mix, p['ln2_g'][l], p['ln2_b'][l])
    x = layer_norm(ALPHA * x + 0.5 * swiglu(x, p['ffn2_w1'][l], p['ffn2_w3'][l], p['ffn2_w2'][l]),
                   p['ln3_g'][l], p['ln3_b'][l])
    return x, st


def setup_inputs(seed: int = 0) -> dict:
    key = jax.random.key(seed)
    ks = jax.random.split(key, 32)
    f32 = jnp.float32

    def nrm(k, shape, scale):
        return jax.random.normal(k, shape, f32) * scale

    dt0 = jnp.exp(jax.random.uniform(ks[23], (DEPTH, SSD_HEADS), f32, math.log(1e-3), math.log(1e-1)))
    return {
        'x_prompt': nrm(ks[0], (BATCH, SEQ, D_MODEL), 1.0),
        'x_sample': nrm(ks[1], (DEC_BATCH, DEC_SEQ, D_MODEL), 1.0),
        'cache_sb_k': nrm(ks[2], (DEPTH, DEC_BATCH, PAST_LEN, SB_HEADS, HEAD_DIM), 1.0),
        'cache_sb_v': nrm(ks[3], (DEPTH, DEC_BATCH, PAST_LEN, SB_HEADS, HEAD_DIM), 1.0),
        'state_short_conv': nrm(ks[4], (DEPTH, DEC_BATCH, SHORT_CONV_W - 1, CONV_DIM), 1.0),
        'state_ssd_conv': nrm(ks[5], (DEPTH, DEC_BATCH, SSD_CONV_W - 1, SSD_CONV_DIM), 1.0),
        'state_ssd': nrm(ks[6], (DEPTH, DEC_BATCH, SSD_HEADS, SSD_HEAD_DIM, SSD_STATE), 0.5),
        'ln1_g': 1.0 + nrm(ks[7], (DEPTH, D_MODEL), 0.02),
        'ln1_b': nrm(ks[8], (DEPTH, D_MODEL), 0.02),
        'ffn1_w1': nrm(ks[9], (DEPTH, D_MODEL, D_FF), D_MODEL ** -0.5),
        'ffn1_w3': nrm(ks[10], (DEPTH, D_MODEL, D_FF), D_MODEL ** -0.5),
        'ffn1_w2': nrm(ks[11], (DEPTH, D_FF, D_MODEL), BETA_INIT * D_FF ** -0.5),
        'w_in': nrm(ks[12], (DEPTH, D_MODEL, D_IN_PROJ), D_MODEL ** -0.5),
        'short_conv_w': nrm(ks[13], (DEPTH, SHORT_CONV_W, CONV_DIM), SHORT_CONV_W ** -0.5),
        'ssd_conv_w': nrm(ks[14], (DEPTH, SSD_CONV_W, SSD_CONV_DIM), SSD_CONV_W ** -0.5),
        'ssd_conv_b': nrm(ks[15], (DEPTH, SSD_CONV_DIM), 0.02),
        'ssd_dt_bias': dt0 + jnp.log(-jnp.expm1(-dt0)),
        'ssd_a_log': jnp.log(jax.random.uniform(ks[16], (DEPTH, SSD_HEADS), f32, 1.0, 16.0)),
        'ssd_d': 1.0 + nrm(ks[17], (DEPTH, SSD_HEADS), 0.1),
        'ssd_norm_g': 1.0 + nrm(ks[18], (DEPTH, SSD_INNER), 0.02),
        'w_out': nrm(ks[19], (DEPTH, D_MIX, D_MODEL), BETA_INIT * D_MIX ** -0.5),
        'ln2_g': 1.0 + nrm(ks[20], (DEPTH, D_MODEL), 0.02),
        'ln2_b': nrm(ks[21], (DEPTH, D_MODEL), 0.02),
        'ffn2_w1': nrm(ks[22], (DEPTH, D_MODEL, D_FF), D_MODEL ** -0.5),
        'ffn2_w3': nrm(ks[24], (DEPTH, D_MODEL, D_FF), D_MODEL ** -0.5),
        'ffn2_w2': nrm(ks[25], (DEPTH, D_FF, D_MODEL), BETA_INIT * D_FF ** -0.5),
        'ln3_g': 1.0 + nrm(ks[26], (DEPTH, D_MODEL), 0.02),
        'ln3_b': nrm(ks[27], (DEPTH, D_MODEL), 0.02),
    }


def reference(x_prompt, x_sample, cache_sb_k, cache_sb_v, state_short_conv, state_ssd_conv, state_ssd,
              ln1_g, ln1_b, ffn1_w1, ffn1_w3, ffn1_w2, w_in, short_conv_w, ssd_conv_w, ssd_conv_b,
              ssd_dt_bias, ssd_a_log, ssd_d, ssd_norm_g, w_out, ln2_g, ln2_b,
              ffn2_w1, ffn2_w3, ffn2_w2, ln3_g, ln3_b):
    p = {'ln1_g': ln1_g, 'ln1_b': ln1_b, 'ffn1_w1': ffn1_w1, 'ffn1_w3': ffn1_w3, 'ffn1_w2': ffn1_w2,
         'w_in': w_in, 'short_conv_w': short_conv_w, 'ssd_conv_w': ssd_conv_w, 'ssd_conv_b': ssd_conv_b,
         'ssd_dt_bias': ssd_dt_bias, 'ssd_a_log': ssd_a_log, 'ssd_d': ssd_d, 'ssd_norm_g': ssd_norm_g,
         'w_out': w_out, 'ln2_g': ln2_g, 'ln2_b': ln2_b, 'ffn2_w1': ffn2_w1, 'ffn2_w3': ffn2_w3,
         'ffn2_w2': ffn2_w2, 'ln3_g': ln3_g, 'ln3_b': ln3_b}
    bp = x_prompt.shape[0]
    dt_ = x_prompt.dtype
    conv0 = jnp.zeros((bp, SHORT_CONV_W - 1, CONV_DIM), dt_)
    ssd_conv0 = jnp.zeros((bp, SSD_CONV_W - 1, SSD_CONV_DIM), dt_)
    ssd0 = jnp.zeros((bp, SSD_HEADS, SSD_HEAD_DIM, SSD_STATE), dt_)
    sample_len = x_sample.shape[1]
    xp, xs = x_prompt, x_sample
    sp, ss = [], []
    for l in range(DEPTH):
        xp, st_p = trunk_layer(xp, p, l, conv0, ssd_conv0, ssd0, CHUNK, stick_breaking_prompt)
        xs, st_s = trunk_layer(xs, p, l, state_short_conv[l], state_ssd_conv[l], state_ssd[l], sample_len,
                               functools.partial(stick_breaking_sample, cache_sb_k[l], cache_sb_v[l]))
        sp.append(st_p)
        ss.append(st_s)
    new_sb_k_prompt = jnp.stack([s[0] for s in sp])
    new_sb_v_prompt = jnp.stack([s[1] for s in sp])
    new_short_conv_prompt = jnp.stack([s[2] for s in sp])
    new_ssd_conv_prompt = jnp.stack([s[3] for s in sp])
    new_ssd_prompt = jnp.stack([s[4] for s in sp])
    new_sb_k_sample = jnp.stack([s[0] for s in ss])
    new_sb_v_sample = jnp.stack([s[1] for s in ss])
    new_short_conv_sample = jnp.stack([s[2] for s in ss])
    new_ssd_conv_sample = jnp.stack([s[3] for s in ss])
    new_ssd_sample = jnp.stack([s[4] for s in ss])
    return (xp, xs, new_sb_k_prompt, new_sb_v_prompt, new_short_conv_prompt, new_ssd_conv_prompt,
            new_ssd_prompt, new_sb_k_sample, new_sb_v_sample, new_short_conv_sample,
            new_ssd_conv_sample, new_ssd_sample)
```

```python
import functools

import jax
import jax.numpy as jnp
from jax import lax
from jax.experimental import pallas as pl
from jax.experimental.pallas import tpu as pltpu

F32 = jnp.float32
BF16 = jnp.bfloat16

D_MODEL = 1024
DEPTH = 4
D_FF = 2816
CONV_DIM = 256
SHORT_CONV_W = 3
SB_HEADS = 4
HEAD_DIM = 64
SB_DIM = SB_HEADS * HEAD_DIM
SSD_HEADS = 8
SSD_HEAD_DIM = 64
SSD_INNER = SSD_HEADS * SSD_HEAD_DIM
SSD_GROUPS = 2
SSD_STATE = 128
SSD_CONV_W = 4
SSD_CONV_DIM = SSD_INNER + 2 * SSD_GROUPS * SSD_STATE
D_MIX = CONV_DIM + SB_DIM + SSD_INNER
D_IN_PROJ = 3 * CONV_DIM + 3 * SB_DIM + SSD_INNER + SSD_CONV_DIM + SSD_HEADS
ALPHA = (2 * DEPTH) ** 0.25
LN_EPS = 1e-5
RMS_EPS = 1e-5

LANES = 128
SUBLANES = 8
HALO = SUBLANES
DT_PAD = LANES
D_IN_PAD = D_IN_PROJ - SSD_HEADS + DT_PAD
TM = 512
TF = D_FF // 2
KEY_BLOCK = 128
EXP_UNDERFLOW = -104.0
VMEM_LIMIT = 56 * 1024 * 1024

_O_BCU = 0
_O_Q = 3 * CONV_DIM
_O_K = _O_Q + SB_DIM
_O_V = _O_K + SB_DIM
_O_Z = _O_V + SB_DIM
_O_XBC = _O_Z + SSD_INNER
_O_DT = _O_XBC + SSD_CONV_DIM


def _sigmoid(x):
    return 1.0 / (1.0 + jnp.exp(-x))


def _softplus(x):
    return jnp.maximum(x, 0.0) + jnp.log1p(jnp.exp(-jnp.abs(x)))


def _layer_norm(y, g, b):
    mu = jnp.mean(y, axis=-1, keepdims=True)
    d = y - mu
    var = jnp.mean(d * d, axis=-1, keepdims=True)
    return d * lax.rsqrt(var + LN_EPS) * g + b


def _split3(x):
    hi = x.astype(BF16)
    r = x - hi.astype(F32)
    mid = r.astype(BF16)
    lo = (r - mid.astype(F32)).astype(BF16)
    return hi, mid, lo


def _dot(a, b):
    return jnp.dot(a, b, preferred_element_type=F32)


def _dot_nt(a, b):
    return lax.dot_general(a, b, (((1,), (1,)), ((), ())), preferred_element_type=F32)


def _dot_tn(a, b):
    return lax.dot_general(a, b, (((0,), (0,)), ((), ())), preferred_element_type=F32)


def _ffn_kernel(x_ref, w1_ref, w3_ref, w2_ref, g_ref, b_ref, o_ref, acc_ref):
    k = pl.program_id(1)
    x = x_ref[...]
    xb = x.astype(BF16)
    h1 = _dot(xb, w1_ref[...])
    h3 = _dot(xb, w3_ref[...])
    hh = (h1 * _sigmoid(h1)) * h3
    part = _dot(hh.astype(BF16), w2_ref[...])

    @pl.when(k == 0)
    def _():
        acc_ref[...] = part

    @pl.when(k > 0)
    def _():
        acc_ref[...] += part

    @pl.when(k == pl.num_programs(1) - 1)
    def _():
        o_ref[...] = _layer_norm(ALPHA * x + 0.5 * acc_ref[...], g_ref[...], b_ref[...])


def _ffn(x, w1, w3, w2, g, b):
    m = x.shape[0]
    return pl.pallas_call(
        _ffn_kernel,
        grid=(m // TM, D_FF // TF),
        in_specs=[
            pl.BlockSpec((TM, D_MODEL), lambda i, k: (i, 0)),
            pl.BlockSpec((D_MODEL, TF), lambda i, k: (0, k)),
            pl.BlockSpec((D_MODEL, TF), lambda i, k: (0, k)),
            pl.BlockSpec((TF, D_MODEL), lambda i, k: (k, 0)),
            pl.BlockSpec((1, D_MODEL), lambda i, k: (0, 0)),
            pl.BlockSpec((1, D_MODEL), lambda i, k: (0, 0)),
        ],
        out_specs=pl.BlockSpec((TM, D_MODEL), lambda i, k: (i, 0)),
        out_shape=jax.ShapeDtypeStruct((m, D_MODEL), F32),
        scratch_shapes=[pltpu.VMEM((TM, D_MODEL), F32)],
        compiler_params=pltpu.CompilerParams(
            dimension_semantics=("arbitrary", "arbitrary"), vmem_limit_bytes=VMEM_LIMIT),
        name="ffn",
    )(x, w1, w3, w2, g, b)


def _inproj_kernel(x_ref, w_ref, bcu_ref, q_ref, k_ref, v_ref, kb_ref, vb_ref, z_ref, xbc_ref, dt_ref):
    xb = x_ref[...].astype(BF16)

    def seg(lo, hi):
        return _dot(xb, w_ref[:, lo:hi])

    bcu_ref[...] = seg(_O_BCU, _O_Q)
    q_ref[...] = (seg(_O_Q, _O_K) * (HEAD_DIM ** -0.5)).astype(BF16)
    k = seg(_O_K, _O_V)
    k_ref[...] = k
    kb_ref[...] = k.astype(BF16)
    v = seg(_O_V, _O_Z)
    v_ref[...] = v
    vb_ref[...] = v.astype(BF16)
    z_ref[...] = seg(_O_Z, _O_XBC)
    xbc_ref[...] = seg(_O_XBC, _O_DT)
    dt_ref[...] = seg(_O_DT, D_IN_PAD)


def _inproj(x, w):
    m = x.shape[0]
    widths = [(3 * CONV_DIM, F32), (SB_DIM, BF16), (SB_DIM, F32), (SB_DIM, F32), (SB_DIM, BF16),
              (SB_DIM, BF16), (SSD_INNER, F32), (SSD_CONV_DIM, F32), (DT_PAD, F32)]
    return pl.pallas_call(
        _inproj_kernel,
        grid=(m // TM,),
        in_specs=[
            pl.BlockSpec((TM, D_MODEL), lambda i: (i, 0)),
            pl.BlockSpec((D_MODEL, D_IN_PAD), lambda i: (0, 0)),
        ],
        out_specs=[pl.BlockSpec((TM, w_), lambda i: (i, 0)) for w_, _ in widths],
        out_shape=[jax.ShapeDtypeStruct((m, w_), d_) for w_, d_ in widths],
        compiler_params=pltpu.CompilerParams(
            dimension_semantics=("arbitrary",), vmem_limit_bytes=VMEM_LIMIT),
        name="inproj",
    )(x, w)


def _mixer_kernel(bcu_ref, z_ref, xbc_ref, dt_ref, sc0_ref, xc0_ref, h0_ref,
                  scw_ref, xcw_ref, xcb_ref, dtb_ref, alog_ref, dvec_ref, ng_ref, ltri_ref,
                  yconv_ref, yssd_ref, sctail_ref, hout_ref,
                  ext_s, ext_x, h_scr, *, chunk):
    c = pl.program_id(1)

    @pl.when(c == 0)
    def _():
        ext_s[0:HALO, :] = sc0_ref[0]
        ext_x[0:HALO, :] = xc0_ref[0]
        h_scr[...] = h0_ref[0]

    bcu = bcu_ref[0]
    b_gate = bcu[:, 0:CONV_DIM]
    cu = bcu[:, CONV_DIM:2 * CONV_DIM] * bcu[:, 2 * CONV_DIM:3 * CONV_DIM]
    ext_s[HALO:HALO + chunk, :] = cu
    conv = None
    for t in range(SHORT_CONV_W):
        term = ext_s[pl.ds(HALO - (SHORT_CONV_W - 1) + t, chunk), :] * scw_ref[t:t + 1, :]
        conv = term if conv is None else conv + term
    yconv_ref[0] = b_gate * conv
    tail_s = ext_s[chunk:chunk + HALO, :]
    ext_s[0:HALO, :] = tail_s
    sctail_ref[0] = tail_s

    ext_x[HALO:HALO + chunk, :] = xbc_ref[0]
    xc = None
    for t in range(SSD_CONV_W):
        term = ext_x[pl.ds(HALO - (SSD_CONV_W - 1) + t, chunk), :] * xcw_ref[t:t + 1, :]
        xc = term if xc is None else xc + term
    xc = xc + xcb_ref[...]
    xc = xc * _sigmoid(xc)
    ext_x[0:HALO, :] = ext_x[chunk:chunk + HALO, :]

    xs = xc[:, 0:SSD_INNER]
    b_all = xc[:, SSD_INNER:SSD_INNER + SSD_GROUPS * SSD_STATE]
    c_all = xc[:, SSD_INNER + SSD_GROUPS * SSD_STATE:SSD_CONV_DIM]

    dt = _softplus(dt_ref[0] + dtb_ref[...])
    a = -jnp.exp(alog_ref[...])
    da_hi, da_mid, da_lo = _split3(dt * a)
    ltri = ltri_ref[...]
    a_cum = _dot(ltri, da_hi) + _dot(ltri, da_mid) + _dot(ltri, da_lo)
    a_cum_t = a_cum.T
    dt_t = dt.T

    row = lax.broadcasted_iota(jnp.int32, (chunk, chunk), 0)
    col = lax.broadcasted_iota(jnp.int32, (chunk, chunk), 1)
    causal = row >= col
    heads_per_group = SSD_HEADS // SSD_GROUPS

    ys = []
    for g in range(SSD_GROUPS):
        bg = b_all[:, g * SSD_STATE:(g + 1) * SSD_STATE]
        cg = c_all[:, g * SSD_STATE:(g + 1) * SSD_STATE]
        bgb = bg.astype(BF16)
        gram = _dot_nt(cg.astype(BF16), bgb)
        for hh in range(heads_per_group):
            h = g * heads_per_group + hh
            acol = a_cum[:, h:h + 1]
            arow = a_cum_t[h:h + 1, :]
            dtrow = dt_t[h:h + 1, :]
            dtcol = dt[:, h:h + 1]
            seg = jnp.where(causal, acol - arow, 0.0)
            decay = jnp.where(causal, jnp.exp(seg), 0.0)
            scores = (gram * decay * dtrow).astype(BF16)
            xh = xs[:, h * SSD_HEAD_DIM:(h + 1) * SSD_HEAD_DIM]
            y = _dot(scores, xh.astype(BF16))
            h_prev = h_scr[h]
            cw = (cg * jnp.exp(acol)).astype(BF16)
            y = y + _dot_nt(cw, h_prev.astype(BF16))
            alast = a_cum[chunk - 1:chunk, h:h + 1]
            xw = (xh * (jnp.exp(alast - acol) * dtcol)).astype(BF16)
            h_scr[h] = jnp.exp(alast) * h_prev + _dot_tn(xw, bgb)
            y = y + dvec_ref[:, h * SSD_HEAD_DIM:(h + 1) * SSD_HEAD_DIM] * xh
            ys.append(y)
    y_all = jnp.concatenate(ys, axis=-1)
    zz = z_ref[0]
    y_all = y_all * (zz * _sigmoid(zz))
    ms = jnp.mean(y_all * y_all, axis=-1, keepdims=True)
    yssd_ref[0] = y_all * lax.rsqrt(ms + RMS_EPS) * ng_ref[...]
    hout_ref[0] = h_scr[...]


def _mixer(bcu, z, xbc, dt, sc0, xc0, h0, scw, xcw, xcb, dtb, alog, dvec, ng, chunk):
    bsz, length, _ = bcu.shape
    ltri = jnp.tril(jnp.ones((chunk, chunk), F32)).astype(BF16)

    def tok(width):
        return pl.BlockSpec((1, chunk, width), lambda b, c: (b, c, 0))

    def per_batch(*dims):
        return pl.BlockSpec((1,) + dims, lambda b, c: (b,) + (0,) * len(dims))

    def whole(arr):
        return pl.BlockSpec(arr.shape, lambda b, c: (0,) * arr.ndim)

    return pl.pallas_call(
        functools.partial(_mixer_kernel, chunk=chunk),
        grid=(bsz, length // chunk),
        in_specs=[tok(3 * CONV_DIM), tok(SSD_INNER), tok(SSD_CONV_DIM), tok(DT_PAD),
                  per_batch(HALO, CONV_DIM), per_batch(HALO, SSD_CONV_DIM),
                  per_batch(SSD_HEADS, SSD_HEAD_DIM, SSD_STATE),
                  whole(scw), whole(xcw), whole(xcb), whole(dtb), whole(alog), whole(dvec), whole(ng),
                  whole(ltri)],
        out_specs=[tok(CONV_DIM), tok(SSD_INNER), per_batch(HALO, CONV_DIM),
                   per_batch(SSD_HEADS, SSD_HEAD_DIM, SSD_STATE)],
        out_shape=[jax.ShapeDtypeStruct((bsz, length, CONV_DIM), F32),
                   jax.ShapeDtypeStruct((bsz, length, SSD_INNER), F32),
                   jax.ShapeDtypeStruct((bsz, HALO, CONV_DIM), F32),
                   jax.ShapeDtypeStruct((bsz, SSD_HEADS, SSD_HEAD_DIM, SSD_STATE), F32)],
        scratch_shapes=[pltpu.VMEM((HALO + chunk, CONV_DIM), F32),
                        pltpu.VMEM((HALO + chunk, SSD_CONV_DIM), F32),
                        pltpu.VMEM((SSD_HEADS, SSD_HEAD_DIM, SSD_STATE), F32)],
        compiler_params=pltpu.CompilerParams(
            dimension_semantics=("arbitrary", "arbitrary"), vmem_limit_bytes=VMEM_LIMIT),
        name="mixer",
    )(bcu, z, xbc, dt, sc0, xc0, h0, scw, xcw, xcb, dtb, alog, dvec, ng, ltri)


def _attn_kernel(q_ref, kd_ref, vd_ref, kp_ref, vp_ref, ud_ref, up_ref, ones_d_ref, ones_p_ref,
                 o_ref, carry_ref, acc_ref, *, tq, past_is_prefix):
    i = pl.program_id(1)
    carry_ref[...] = jnp.zeros_like(carry_ref)
    acc_ref[...] = jnp.zeros_like(acc_ref)
    q = q_ref[0]
    lane = lax.broadcasted_iota(jnp.int32, (1, LANES), 1)
    heads_per_tile = LANES // HEAD_DIM

    def visit(kblk, vblk, u_ref, ones_ref, mask):
        nk = kblk.shape[0]
        kblk = kblk.astype(BF16)
        vblk = vblk.astype(BF16)
        for p in range(SB_DIM // LANES):
            kp = kblk[:, p * LANES:(p + 1) * LANES]
            vp = vblk[:, p * LANES:(p + 1) * LANES]
            qp = q[:, p * LANES:(p + 1) * LANES]
            for s in range(heads_per_tile):
                h = p * heads_per_tile + s
                in_head = (lane // HEAD_DIM) == s
                z = _dot_nt(jnp.where(in_head, qp, jnp.zeros_like(qp)), kp)
                lg = -_softplus(z)
                if mask is not None:
                    lg = jnp.where(mask, lg, 0.0)
                parts = _split3(lg)
                u = u_ref[...]
                ones = ones_ref[...]
                carry = carry_ref[h]
                cs = sum(_dot(pt, u) for pt in parts) + carry[:, 0:nk]
                tot = sum(_dot(pt, ones) for pt in parts)
                w = jnp.exp(z + cs)
                if mask is not None:
                    w = jnp.where(mask, w, 0.0)
                pv = _dot(w.astype(BF16), vp)
                acc_ref[:, p * LANES:(p + 1) * LANES] += jnp.where(in_head, pv, 0.0)
                carry_ref[h] = carry + tot

    def worst_carry():
        m = carry_ref[0]
        for h in range(1, SB_HEADS):
            m = jnp.maximum(m, carry_ref[h])
        return jnp.max(m)

    row = lax.broadcasted_iota(jnp.int32, (tq, tq), 0)
    col = lax.broadcasted_iota(jnp.int32, (tq, tq), 1)
    visit(kd_ref[0], vd_ref[0], ud_ref, ones_d_ref, col < row)

    n_past = i * (tq // KEY_BLOCK) if past_is_prefix else kp_ref.shape[1] // KEY_BLOCK

    def cond(state):
        kb, worst = state
        return jnp.logical_and(kb >= 0, worst >= EXP_UNDERFLOW)

    def body(state):
        kb, _ = state
        start = pl.multiple_of(kb * KEY_BLOCK, KEY_BLOCK)
        visit(kp_ref[0, pl.ds(start, KEY_BLOCK), :], vp_ref[0, pl.ds(start, KEY_BLOCK), :],
              up_ref, ones_p_ref, None)
        return kb - 1, worst_carry()

    lax.while_loop(cond, body, (n_past - 1, worst_carry()))
    o_ref[0] = acc_ref[...]


def _suffix_matrix(n):
    return jnp.tril(jnp.ones((n, n), F32)).astype(BF16)


def _attn(q, k_new, v_new, k_past, v_past, tq, past_is_prefix):
    bsz, length, _ = q.shape
    past_len = k_past.shape[1]
    ud = _suffix_matrix(tq)
    up = _suffix_matrix(KEY_BLOCK)
    ones_d = jnp.ones((tq, LANES), BF16)
    ones_p = jnp.ones((KEY_BLOCK, LANES), BF16)

    def tile():
        return pl.BlockSpec((1, tq, SB_DIM), lambda b, i: (b, i, 0))

    def past():
        return pl.BlockSpec((1, past_len, SB_DIM), lambda b, i: (b, 0, 0))

    def whole(arr):
        return pl.BlockSpec(arr.shape, lambda b, i: (0,) * arr.ndim)

    return pl.pallas_call(
        functools.partial(_attn_kernel, tq=tq, past_is_prefix=past_is_prefix),
        grid=(bsz, length // tq),
        in_specs=[tile(), tile(), tile(), past(), past(), whole(ud), whole(up), whole(ones_d), whole(ones_p)],
        out_specs=tile(),
        out_shape=jax.ShapeDtypeStruct((bsz, length, SB_DIM), F32),
        scratch_shapes=[pltpu.VMEM((SB_HEADS, tq, LANES), F32), pltpu.VMEM((tq, SB_DIM), F32)],
        compiler_params=pltpu.CompilerParams(
            dimension_semantics=("arbitrary", "arbitrary"), vmem_limit_bytes=VMEM_LIMIT),
        name="attn",
    )(q, k_new, v_new, k_past, v_past, ud, up, ones_d, ones_p)


def _outproj_kernel(x_ref, yc_ref, ysb_ref, yssd_ref, w_ref, g_ref, b_ref, o_ref):
    mix_in = jnp.concatenate([yc_ref[...], ysb_ref[...], yssd_ref[...]], axis=-1).astype(BF16)
    mix = _dot(mix_in, w_ref[...])
    o_ref[...] = _layer_norm(ALPHA * x_ref[...] + mix, g_ref[...], b_ref[...])


def _outproj(x, yc, ysb, yssd, w, g, b):
    m = x.shape[0]

    def tok(width):
        return pl.BlockSpec((TM, width), lambda i: (i, 0))

    return pl.pallas_call(
        _outproj_kernel,
        grid=(m // TM,),
        in_specs=[tok(D_MODEL), tok(CONV_DIM), tok(SB_DIM), tok(SSD_INNER),
                  pl.BlockSpec((D_MIX, D_MODEL), lambda i: (0, 0)),
                  pl.BlockSpec((1, D_MODEL), lambda i: (0, 0)),
                  pl.BlockSpec((1, D_MODEL), lambda i: (0, 0))],
        out_specs=tok(D_MODEL),
        out_shape=jax.ShapeDtypeStruct((m, D_MODEL), F32),
        compiler_params=pltpu.CompilerParams(
            dimension_semantics=("arbitrary",), vmem_limit_bytes=VMEM_LIMIT),
        name="outproj",
    )(x, yc, ysb, yssd, w, g, b)


def _pad_rows_front(state):
    return jnp.pad(state, ((0, 0), (HALO - state.shape[1], 0), (0, 0)))


def _layer(x, bsz, length, lp, conv_prev, ssd_conv_prev, ssd_h0, chunk, tq, k_past, v_past):
    x = _ffn(x, lp['ffn1_w1'], lp['ffn1_w3'], lp['ffn1_w2'], lp['ln1_g'], lp['ln1_b'])
    bcu, q, k, v, kb, vb, z, xbc, dt = _inproj(x, lp['w_in'])

    def r3(a):
        return a.reshape(bsz, length, a.shape[-1])

    xbc3 = r3(xbc)
    yconv, yssd, sctail, h_new = _mixer(
        r3(bcu), r3(z), xbc3, r3(dt), _pad_rows_front(conv_prev), _pad_rows_front(ssd_conv_prev), ssd_h0,
        lp['short_conv_w'], lp['ssd_conv_w'], lp['ssd_conv_b'], lp['ssd_dt_bias'], lp['ssd_a_log'],
        lp['ssd_d'], lp['ssd_norm_g'], chunk)
    if k_past is None:
        ysb = _attn(r3(q), r3(kb), r3(vb), r3(kb), r3(vb), tq, True)
    else:
        ysb = _attn(r3(q), r3(kb), r3(vb), k_past, v_past, tq, False)
    x = _outproj(x, yconv.reshape(-1, CONV_DIM), ysb.reshape(-1, SB_DIM), yssd.reshape(-1, SSD_INNER),
                 lp['w_out'], lp['ln2_g'], lp['ln2_b'])
    x = _ffn(x, lp['ffn2_w1'], lp['ffn2_w3'], lp['ffn2_w2'], lp['ln3_g'], lp['ln3_b'])
    states = (r3(k).reshape(bsz, length, SB_HEADS, HEAD_DIM), r3(v).reshape(bsz, length, SB_HEADS, HEAD_DIM),
              sctail[:, HALO - (SHORT_CONV_W - 1):, :], xbc3[:, length - (SSD_CONV_W - 1):, :], h_new)
    return x, states


def _layer_params(l, ln1_g, ln1_b, ffn1_w1, ffn1_w3, ffn1_w2, w_in, short_conv_w, ssd_conv_w, ssd_conv_b,
                  ssd_dt_bias, ssd_a_log, ssd_d, ssd_norm_g, w_out, ln2_g, ln2_b,
                  ffn2_w1, ffn2_w3, ffn2_w2, ln3_g, ln3_b):
    def row(a):
        return a[l].reshape(1, -1)

    def pad_heads(a):
        return jnp.pad(a[l], (0, DT_PAD - SSD_HEADS)).reshape(1, DT_PAD)

    w = w_in[l]
    w_pad = jnp.pad(w, ((0, 0), (0, D_IN_PAD - D_IN_PROJ))).astype(BF16)
    return {
        'ln1_g': row(ln1_g), 'ln1_b': row(ln1_b), 'ln2_g': row(ln2_g), 'ln2_b': row(ln2_b),
        'ln3_g': row(ln3_g), 'ln3_b': row(ln3_b),
        'ffn1_w1': ffn1_w1[l].astype(BF16), 'ffn1_w3': ffn1_w3[l].astype(BF16), 'ffn1_w2': ffn1_w2[l].astype(BF16),
        'ffn2_w1': ffn2_w1[l].astype(BF16), 'ffn2_w3': ffn2_w3[l].astype(BF16), 'ffn2_w2': ffn2_w2[l].astype(BF16),
        'w_in': w_pad, 'w_out': w_out[l].astype(BF16),
        'short_conv_w': short_conv_w[l], 'ssd_conv_w': ssd_conv_w[l], 'ssd_conv_b': row(ssd_conv_b),
        'ssd_dt_bias': pad_heads(ssd_dt_bias), 'ssd_a_log': pad_heads(ssd_a_log),
        'ssd_d': jnp.repeat(ssd_d[l], SSD_HEAD_DIM).reshape(1, SSD_INNER), 'ssd_norm_g': row(ssd_norm_g),
    }


def kernel(x_prompt, x_sample, cache_sb_k, cache_sb_v, state_short_conv, state_ssd_conv, state_ssd,
           ln1_g, ln1_b, ffn1_w1, ffn1_w3, ffn1_w2, w_in, short_conv_w, ssd_conv_w, ssd_conv_b,
           ssd_dt_bias, ssd_a_log, ssd_d, ssd_norm_g, w_out, ln2_g, ln2_b,
           ffn2_w1, ffn2_w3, ffn2_w2, ln3_g, ln3_b):
    bp, lp_, _ = x_prompt.shape
    bs, ls, _ = x_sample.shape
    past_len = cache_sb_k.shape[2]
    conv0 = jnp.zeros((bp, SHORT_CONV_W - 1, CONV_DIM), F32)
    ssd_conv0 = jnp.zeros((bp, SSD_CONV_W - 1, SSD_CONV_DIM), F32)
    ssd0 = jnp.zeros((bp, SSD_HEADS, SSD_HEAD_DIM, SSD_STATE), F32)
    xp = x_prompt.reshape(bp * lp_, D_MODEL)
    xs = x_sample.reshape(bs * ls, D_MODEL)
    sp, ss = [], []
    for l in range(DEPTH):
        lp = _layer_params(l, ln1_g, ln1_b, ffn1_w1, ffn1_w3, ffn1_w2, w_in, short_conv_w, ssd_conv_w,
                           ssd_conv_b, ssd_dt_bias, ssd_a_log, ssd_d, ssd_norm_g, w_out, ln2_g, ln2_b,
                           ffn2_w1, ffn2_w3, ffn2_w2, ln3_g, ln3_b)
        xp, st_p = _layer(xp, bp, lp_, lp, conv0, ssd_conv0, ssd0, 128, 128, None, None)
        xs, st_s = _layer(xs, bs, ls, lp, state_short_conv[l], state_ssd_conv[l], state_ssd[l], ls, ls,
                          cache_sb_k[l].reshape(bs, past_len, SB_DIM), cache_sb_v[l].reshape(bs, past_len, SB_DIM))
        sp.append(st_p)
        ss.append(st_s)
    outs = [xp.reshape(bp, lp_, D_MODEL), xs.reshape(bs, ls, D_MODEL)]
    for group in (sp, ss):
        for j in range(5):
            outs.append(jnp.stack([s[j] for s in group]))
    return tuple(outs)
```

```python
import functools

import jax
import jax.numpy as jnp
from jax import lax
from jax.experimental import pallas as pl
from jax.experimental.pallas import tpu as pltpu

F32 = jnp.float32
BF16 = jnp.bfloat16

D_MODEL = 1024
DEPTH = 4
D_FF = 2816
CONV_DIM = 256
SHORT_CONV_W = 3
SB_HEADS = 4
HEAD_DIM = 64
SB_DIM = SB_HEADS * HEAD_DIM
SSD_HEADS = 8
SSD_HEAD_DIM = 64
SSD_INNER = SSD_HEADS * SSD_HEAD_DIM
SSD_GROUPS = 2
SSD_STATE = 128
SSD_CONV_W = 4
SSD_CONV_DIM = SSD_INNER + 2 * SSD_GROUPS * SSD_STATE
D_MIX = CONV_DIM + SB_DIM + SSD_INNER
D_IN_PROJ = 3 * CONV_DIM + 3 * SB_DIM + SSD_INNER + SSD_CONV_DIM + SSD_HEADS
ALPHA = (2 * DEPTH) ** 0.25
LN_EPS = 1e-5
RMS_EPS = 1e-5

LANES = 128
SUBLANES = 8
HALO = SUBLANES
DT_PAD = LANES
D_IN_PAD = D_IN_PROJ - SSD_HEADS + DT_PAD
TM = 512
TF = 256
KEY_BLOCK = 256
PROMPT_Q_TILE = 256
PROMPT_CHUNK = 128
EXP_UNDERFLOW = -104.0
VMEM_LIMIT = 56 * 1024 * 1024

_O_BCU = 0
_O_Q = 3 * CONV_DIM
_O_K = _O_Q + SB_DIM
_O_V = _O_K + SB_DIM
_O_Z = _O_V + SB_DIM
_O_XBC = _O_Z + SSD_INNER
_O_DT = _O_XBC + SSD_CONV_DIM


def _sigmoid(x):
    return 0.5 * jnp.tanh(0.5 * x) + 0.5


def _silu(x):
    return x * _sigmoid(x)


def _softplus(x):
    return jnp.maximum(x, 0.0) + jnp.log1p(jnp.exp(-jnp.abs(x)))


def _log_sigmoid_neg(x):
    return jnp.minimum(-x, 0.0) - jnp.log(1.0 + jnp.exp(-jnp.abs(x)))


def _layer_norm(y, g, b):
    mu = jnp.mean(y, axis=-1, keepdims=True)
    d = y - mu
    var = jnp.mean(d * d, axis=-1, keepdims=True)
    return d * lax.rsqrt(var + LN_EPS) * g + b


def _split3(x):
    hi = x.astype(BF16)
    r = x - hi.astype(F32)
    mid = r.astype(BF16)
    lo = (r - mid.astype(F32)).astype(BF16)
    return hi, mid, lo


def _dot(a, b):
    return jnp.dot(a, b, preferred_element_type=F32)


def _dot_nt(a, b):
    return lax.dot_general(a, b, (((1,), (1,)), ((), ())), preferred_element_type=F32)


def _dot_tn(a, b):
    return lax.dot_general(a, b, (((0,), (0,)), ((), ())), preferred_element_type=F32)


def _ffn_block(x, w1_ref, w3_ref, w2_ref, g_ref, b_ref):
    xb = x.astype(BF16)
    acc = None
    for c in range(D_FF // TF):
        cols = slice(c * TF, (c + 1) * TF)
        h1 = _dot(xb, w1_ref[:, cols])
        h3 = _dot(xb, w3_ref[:, cols])
        part = _dot((_silu(h1) * h3).astype(BF16), w2_ref[cols, :])
        acc = part if acc is None else acc + part
    return _layer_norm(ALPHA * x + 0.5 * acc, g_ref[...], b_ref[...])


def _ffn_kernel(x_ref, w1_ref, w3_ref, w2_ref, g_ref, b_ref, o_ref):
    o_ref[...] = _ffn_block(x_ref[...], w1_ref, w3_ref, w2_ref, g_ref, b_ref)


def _mix_ffn_kernel(x_ref, yc_ref, ysb_ref, yssd_ref, wo_ref, g2_ref, b2_ref,
                    w1_ref, w3_ref, w2_ref, g3_ref, b3_ref, o_ref):
    mix_in = jnp.concatenate([yc_ref[...], ysb_ref[...], yssd_ref[...]], axis=-1).astype(BF16)
    x = _layer_norm(ALPHA * x_ref[...] + _dot(mix_in, wo_ref[...]), g2_ref[...], b2_ref[...])
    o_ref[...] = _ffn_block(x, w1_ref, w3_ref, w2_ref, g3_ref, b3_ref)


def _resident(shape):
    return pl.BlockSpec(shape, lambda *_: (0,) * len(shape), pipeline_mode=pl.Buffered(1))


def _tokens(width):
    return pl.BlockSpec((TM, width), lambda i: (i, 0))


_FFN_WEIGHT_SPECS = [_resident((D_MODEL, D_FF)), _resident((D_MODEL, D_FF)), _resident((D_FF, D_MODEL)),
                     _resident((1, D_MODEL)), _resident((1, D_MODEL))]
_DENSE_PARAMS = pltpu.CompilerParams(dimension_semantics=("arbitrary",), vmem_limit_bytes=VMEM_LIMIT)


def _ffn(x, w1, w3, w2, g, b):
    m = x.shape[0]
    return pl.pallas_call(
        _ffn_kernel,
        grid=(m // TM,),
        in_specs=[_tokens(D_MODEL)] + _FFN_WEIGHT_SPECS,
        out_specs=_tokens(D_MODEL),
        out_shape=jax.ShapeDtypeStruct((m, D_MODEL), F32),
        compiler_params=_DENSE_PARAMS,
        name="ffn",
    )(x, w1, w3, w2, g, b)


def _mix_ffn(x, yc, ysb, yssd, wo, g2, b2, w1, w3, w2, g3, b3):
    m = x.shape[0]
    return pl.pallas_call(
        _mix_ffn_kernel,
        grid=(m // TM,),
        in_specs=[_tokens(D_MODEL), _tokens(CONV_DIM), _tokens(SB_DIM), _tokens(SSD_INNER),
                  _resident((D_MIX, D_MODEL)), _resident((1, D_MODEL)), _resident((1, D_MODEL))]
                 + _FFN_WEIGHT_SPECS,
        out_specs=_tokens(D_MODEL),
        out_shape=jax.ShapeDtypeStruct((m, D_MODEL), F32),
        compiler_params=_DENSE_PARAMS,
        name="mix_ffn",
    )(x, yc, ysb, yssd, wo, g2, b2, w1, w3, w2, g3, b3)


def _inproj_kernel(x_ref, w_ref, bcu_ref, q_ref, k_ref, v_ref, kb_ref, vb_ref, z_ref, xbc_ref, dt_ref):
    xb = x_ref[...].astype(BF16)

    def seg(lo, hi):
        return _dot(xb, w_ref[:, lo:hi])

    bcu_ref[...] = seg(_O_BCU, _O_Q)
    q_ref[...] = (seg(_O_Q, _O_K) * (HEAD_DIM ** -0.5)).astype(BF16)
    k = seg(_O_K, _O_V)
    k_ref[...] = k
    kb_ref[...] = k.astype(BF16)
    v = seg(_O_V, _O_Z)
    v_ref[...] = v
    vb_ref[...] = v.astype(BF16)
    z_ref[...] = seg(_O_Z, _O_XBC)
    xbc_ref[...] = seg(_O_XBC, _O_DT)
    dt_ref[...] = seg(_O_DT, D_IN_PAD)


def _inproj(x, w):
    m = x.shape[0]
    widths = [(3 * CONV_DIM, F32), (SB_DIM, BF16), (SB_DIM, F32), (SB_DIM, F32), (SB_DIM, BF16),
              (SB_DIM, BF16), (SSD_INNER, F32), (SSD_CONV_DIM, F32), (DT_PAD, F32)]
    return pl.pallas_call(
        _inproj_kernel,
        grid=(m // TM,),
        in_specs=[
            pl.BlockSpec((TM, D_MODEL), lambda i: (i, 0)),
            _resident((D_MODEL, D_IN_PAD)),
        ],
        out_specs=[pl.BlockSpec((TM, w_), lambda i: (i, 0)) for w_, _ in widths],
        out_shape=[jax.ShapeDtypeStruct((m, w_), d_) for w_, d_ in widths],
        compiler_params=pltpu.CompilerParams(
            dimension_semantics=("arbitrary",), vmem_limit_bytes=VMEM_LIMIT),
        name="inproj",
    )(x, w)


def _mixer_kernel(bcu_ref, z_ref, xbc_ref, dt_ref, sc0_ref, xc0_ref, h0_ref,
                  scw_ref, xcw_ref, xcb_ref, dtb_ref, alog_ref, dvec_ref, ng_ref, ltri_ref,
                  yconv_ref, yssd_ref, sctail_ref, hout_ref,
                  ext_s, ext_x, h_scr, *, chunk):
    c = pl.program_id(1)

    @pl.when(c == 0)
    def _():
        ext_s[0:HALO, :] = sc0_ref[0]
        ext_x[0:HALO, :] = xc0_ref[0]
        h_scr[...] = h0_ref[0]

    bcu = bcu_ref[0]
    b_gate = bcu[:, 0:CONV_DIM]
    cu = bcu[:, CONV_DIM:2 * CONV_DIM] * bcu[:, 2 * CONV_DIM:3 * CONV_DIM]
    ext_s[HALO:HALO + chunk, :] = cu
    conv = None
    for t in range(SHORT_CONV_W):
        term = ext_s[pl.ds(HALO - (SHORT_CONV_W - 1) + t, chunk), :] * scw_ref[t:t + 1, :]
        conv = term if conv is None else conv + term
    yconv_ref[0] = b_gate * conv
    tail_s = ext_s[chunk:chunk + HALO, :]
    ext_s[0:HALO, :] = tail_s
    sctail_ref[0] = tail_s

    ext_x[HALO:HALO + chunk, :] = xbc_ref[0]
    xc = None
    for t in range(SSD_CONV_W):
        term = ext_x[pl.ds(HALO - (SSD_CONV_W - 1) + t, chunk), :] * xcw_ref[t:t + 1, :]
        xc = term if xc is None else xc + term
    xc = xc + xcb_ref[...]
    xc = _silu(xc)
    ext_x[0:HALO, :] = ext_x[chunk:chunk + HALO, :]

    xs = xc[:, 0:SSD_INNER]
    b_all = xc[:, SSD_INNER:SSD_INNER + SSD_GROUPS * SSD_STATE]
    c_all = xc[:, SSD_INNER + SSD_GROUPS * SSD_STATE:SSD_CONV_DIM]

    dt = _softplus(dt_ref[0] + dtb_ref[...])
    a = -jnp.exp(alog_ref[...])
    da_hi, da_mid, da_lo = _split3(dt * a)
    ltri = ltri_ref[...]
    a_cum = _dot(ltri, da_hi) + _dot(ltri, da_mid) + _dot(ltri, da_lo)
    a_cum_t = a_cum.T
    dt_t = dt.T

    row = lax.broadcasted_iota(jnp.int32, (chunk, chunk), 0)
    col = lax.broadcasted_iota(jnp.int32, (chunk, chunk), 1)
    causal = row >= col
    heads_per_group = SSD_HEADS // SSD_GROUPS

    ys = []
    for g in range(SSD_GROUPS):
        bg = b_all[:, g * SSD_STATE:(g + 1) * SSD_STATE]
        cg = c_all[:, g * SSD_STATE:(g + 1) * SSD_STATE]
        bgb = bg.astype(BF16)
        gram = _dot_nt(cg.astype(BF16), bgb)
        for hh in range(heads_per_group):
            h = g * heads_per_group + hh
            acol = a_cum[:, h:h + 1]
            arow = a_cum_t[h:h + 1, :]
            dtrow = dt_t[h:h + 1, :]
            dtcol = dt[:, h:h + 1]
            seg = jnp.where(causal, acol - arow, 0.0)
            decay = jnp.where(causal, jnp.exp(seg), 0.0)
            scores = (gram * decay * dtrow).astype(BF16)
            xh = xs[:, h * SSD_HEAD_DIM:(h + 1) * SSD_HEAD_DIM]
            y = _dot(scores, xh.astype(BF16))
            h_prev = h_scr[h]
            cw = (cg * jnp.exp(acol)).astype(BF16)
            y = y + _dot_nt(cw, h_prev.astype(BF16))
            alast = a_cum[chunk - 1:chunk, h:h + 1]
            xw = (xh * (jnp.exp(alast - acol) * dtcol)).astype(BF16)
            h_scr[h] = jnp.exp(alast) * h_prev + _dot_tn(xw, bgb)
            y = y + dvec_ref[:, h * SSD_HEAD_DIM:(h + 1) * SSD_HEAD_DIM] * xh
            ys.append(y)
    y_all = jnp.concatenate(ys, axis=-1)
    zz = z_ref[0]
    y_all = y_all * _silu(zz)
    ms = jnp.mean(y_all * y_all, axis=-1, keepdims=True)
    yssd_ref[0] = y_all * lax.rsqrt(ms + RMS_EPS) * ng_ref[...]
    hout_ref[0] = h_scr[...]


def _mixer(bcu, z, xbc, dt, sc0, xc0, h0, scw, xcw, xcb, dtb, alog, dvec, ng, chunk):
    bsz, length, _ = bcu.shape
    ltri = jnp.tril(jnp.ones((chunk, chunk), F32)).astype(BF16)

    def tok(width):
        return pl.BlockSpec((1, chunk, width), lambda b, c: (b, c, 0))

    def per_batch(*dims):
        return pl.BlockSpec((1,) + dims, lambda b, c: (b,) + (0,) * len(dims))

    def whole(arr):
        return pl.BlockSpec(arr.shape, lambda b, c: (0,) * arr.ndim)

    return pl.pallas_call(
        functools.partial(_mixer_kernel, chunk=chunk),
        grid=(bsz, length // chunk),
        in_specs=[tok(3 * CONV_DIM), tok(SSD_INNER), tok(SSD_CONV_DIM), tok(DT_PAD),
                  per_batch(HALO, CONV_DIM), per_batch(HALO, SSD_CONV_DIM),
                  per_batch(SSD_HEADS, SSD_HEAD_DIM, SSD_STATE),
                  whole(scw), whole(xcw), whole(xcb), whole(dtb), whole(alog), whole(dvec), whole(ng),
                  whole(ltri)],
        out_specs=[tok(CONV_DIM), tok(SSD_INNER), per_batch(HALO, CONV_DIM),
                   per_batch(SSD_HEADS, SSD_HEAD_DIM, SSD_STATE)],
        out_shape=[jax.ShapeDtypeStruct((bsz, length, CONV_DIM), F32),
                   jax.ShapeDtypeStruct((bsz, length, SSD_INNER), F32),
                   jax.ShapeDtypeStruct((bsz, HALO, CONV_DIM), F32),
                   jax.ShapeDtypeStruct((bsz, SSD_HEADS, SSD_HEAD_DIM, SSD_STATE), F32)],
        scratch_shapes=[pltpu.VMEM((HALO + chunk, CONV_DIM), F32),
                        pltpu.VMEM((HALO + chunk, SSD_CONV_DIM), F32),
                        pltpu.VMEM((SSD_HEADS, SSD_HEAD_DIM, SSD_STATE), F32)],
        compiler_params=pltpu.CompilerParams(
            dimension_semantics=("arbitrary", "arbitrary"), vmem_limit_bytes=VMEM_LIMIT),
        name="mixer",
    )(bcu, z, xbc, dt, sc0, xc0, h0, scw, xcw, xcb, dtb, alog, dvec, ng, ltri)


def _attn_kernel(q_ref, kd_ref, vd_ref, kp_ref, vp_ref, rd_ref, rp_ref,
                 o_ref, carry_ref, acc_ref, *, tq, past_is_prefix):
    i = pl.program_id(1)
    n_pairs = SB_DIM // LANES
    q = q_ref[0]
    lane = lax.broadcasted_iota(jnp.int32, (1, LANES), 1)
    low_head = lane < HEAD_DIM
    q_stack = []
    for p in range(n_pairs):
        qp = q[:, p * LANES:(p + 1) * LANES]
        zero = jnp.zeros_like(qp)
        q_stack.append(jnp.concatenate([jnp.where(low_head, qp, zero), jnp.where(low_head, zero, qp)], axis=0))

    def visit(kblk, vblk, r_ref, mask, first):
        nk = kblk.shape[0]
        kblk = kblk.astype(BF16)
        vblk = vblk.astype(BF16)
        z = jnp.concatenate([_dot_nt(q_stack[p], kblk[:, p * LANES:(p + 1) * LANES])
                             for p in range(n_pairs)], axis=0)
        lg = _log_sigmoid_neg(z)
        if mask is not None:
            lg = jnp.where(mask, lg, 0.0)
        hi = lg.astype(BF16)
        lo = (lg - hi.astype(F32)).astype(BF16)
        r = r_ref[...]
        res = _dot(hi, r) + _dot(lo, r)
        tot = res[:, 0:LANES]
        cs = res[:, LANES:LANES + nk]
        if not first:
            carry = carry_ref[...]
            tot = tot + carry
            cs = cs + (carry[:, 0:nk] if nk <= LANES else jnp.concatenate([carry] * (nk // LANES), axis=1))
        w = jnp.exp(z + cs)
        if mask is not None:
            w = jnp.where(mask, w, 0.0)
        wb = w.astype(BF16)
        outs = []
        for p in range(n_pairs):
            pv = _dot(wb[2 * p * tq:(2 * p + 2) * tq], vblk[:, p * LANES:(p + 1) * LANES])
            outs.append(jnp.where(low_head, pv[0:tq], pv[tq:2 * tq]))
        out = jnp.concatenate(outs, axis=1)
        acc_ref[...] = out if first else acc_ref[...] + out
        carry_ref[...] = tot
        return jnp.max(tot)

    row = lax.broadcasted_iota(jnp.int32, (SB_HEADS * tq, tq), 0) & (tq - 1)
    col = lax.broadcasted_iota(jnp.int32, (SB_HEADS * tq, tq), 1)
    worst = visit(kd_ref[0], vd_ref[0], rd_ref, col < row, True)

    n_past = i * (tq // KEY_BLOCK) if past_is_prefix else kp_ref.shape[1] // KEY_BLOCK

    def cond(state):
        kb, worst_carry = state
        return jnp.logical_and(kb >= 0, worst_carry >= EXP_UNDERFLOW)

    def body(state):
        kb, _ = state
        start = pl.multiple_of(kb * KEY_BLOCK, KEY_BLOCK)
        return kb - 1, visit(kp_ref[0, pl.ds(start, KEY_BLOCK), :], vp_ref[0, pl.ds(start, KEY_BLOCK), :],
                             rp_ref, None, False)

    lax.while_loop(cond, body, (n_past - 1, worst))
    o_ref[0] = acc_ref[...]


def _suffix_sum_rhs(n):
    return jnp.concatenate([jnp.ones((n, LANES), F32), jnp.tril(jnp.ones((n, n), F32))], axis=1).astype(BF16)


def _attn(q, k_new, v_new, k_past, v_past, tq, past_is_prefix):
    bsz, length, _ = q.shape
    past_len = k_past.shape[1]
    assert tq & (tq - 1) == 0 and past_len % KEY_BLOCK == 0
    assert not past_is_prefix or tq % KEY_BLOCK == 0
    rd = _suffix_sum_rhs(tq)
    rp = _suffix_sum_rhs(KEY_BLOCK)

    def tile():
        return pl.BlockSpec((1, tq, SB_DIM), lambda b, i: (b, i, 0))

    def past():
        return pl.BlockSpec((1, past_len, SB_DIM), lambda b, i: (b, 0, 0))

    def whole(arr):
        return pl.BlockSpec(arr.shape, lambda b, i: (0,) * arr.ndim)

    return pl.pallas_call(
        functools.partial(_attn_kernel, tq=tq, past_is_prefix=past_is_prefix),
        grid=(bsz, length // tq),
        in_specs=[tile(), tile(), tile(), past(), past(), whole(rd), whole(rp)],
        out_specs=tile(),
        out_shape=jax.ShapeDtypeStruct((bsz, length, SB_DIM), F32),
        scratch_shapes=[pltpu.VMEM((SB_HEADS * tq, LANES), F32), pltpu.VMEM((tq, SB_DIM), F32)],
        compiler_params=pltpu.CompilerParams(
            dimension_semantics=("arbitrary", "arbitrary"), vmem_limit_bytes=VMEM_LIMIT),
        name="attn",
    )(q, k_new, v_new, k_past, v_past, rd, rp)


def _pad_rows_front(state):
    return jnp.pad(state, ((0, 0), (HALO - state.shape[1], 0), (0, 0)))


def _layer(x, bsz, length, lp, conv_prev, ssd_conv_prev, ssd_h0, chunk, tq, k_past, v_past):
    x = _ffn(x, lp['ffn1_w1'], lp['ffn1_w3'], lp['ffn1_w2'], lp['ln1_g'], lp['ln1_b'])
    bcu, q, k, v, kb, vb, z, xbc, dt = _inproj(x, lp['w_in'])

    def r3(a):
        return a.reshape(bsz, length, a.shape[-1])

    xbc3 = r3(xbc)
    yconv, yssd, sctail, h_new = _mixer(
        r3(bcu), r3(z), xbc3, r3(dt), _pad_rows_front(conv_prev), _pad_rows_front(ssd_conv_prev), ssd_h0,
        lp['short_conv_w'], lp['ssd_conv_w'], lp['ssd_conv_b'], lp['ssd_dt_bias'], lp['ssd_a_log'],
        lp['ssd_d'], lp['ssd_norm_g'], chunk)
    if k_past is None:
        ysb = _attn(r3(q), r3(kb), r3(vb), r3(kb), r3(vb), tq, True)
    else:
        ysb = _attn(r3(q), r3(kb), r3(vb), k_past, v_past, tq, False)
    x = _mix_ffn(x, yconv.reshape(-1, CONV_DIM), ysb.reshape(-1, SB_DIM), yssd.reshape(-1, SSD_INNER),
                 lp['w_out'], lp['ln2_g'], lp['ln2_b'],
                 lp['ffn2_w1'], lp['ffn2_w3'], lp['ffn2_w2'], lp['ln3_g'], lp['ln3_b'])
    states = (r3(k).reshape(bsz, length, SB_HEADS, HEAD_DIM), r3(v).reshape(bsz, length, SB_HEADS, HEAD_DIM),
              sctail[:, HALO - (SHORT_CONV_W - 1):, :], xbc3[:, length - (SSD_CONV_W - 1):, :], h_new)
    return x, states


def _layer_params(l, ln1_g, ln1_b, ffn1_w1, ffn1_w3, ffn1_w2, w_in, short_conv_w, ssd_conv_w, ssd_conv_b,
                  ssd_dt_bias, ssd_a_log, ssd_d, ssd_norm_g, w_out, ln2_g, ln2_b,
                  ffn2_w1, ffn2_w3, ffn2_w2, ln3_g, ln3_b):
    def row(a):
        return a[l].reshape(1, -1)

    def pad_heads(a):
        return jnp.pad(a[l], (0, DT_PAD - SSD_HEADS)).reshape(1, DT_PAD)

    w = w_in[l]
    w_pad = jnp.pad(w, ((0, 0), (0, D_IN_PAD - D_IN_PROJ))).astype(BF16)
    return {
        'ln1_g': row(ln1_g), 'ln1_b': row(ln1_b), 'ln2_g': row(ln2_g), 'ln2_b': row(ln2_b),
        'ln3_g': row(ln3_g), 'ln3_b': row(ln3_b),
        'ffn1_w1': ffn1_w1[l].astype(BF16), 'ffn1_w3': ffn1_w3[l].astype(BF16), 'ffn1_w2': ffn1_w2[l].astype(BF16),
        'ffn2_w1': ffn2_w1[l].astype(BF16), 'ffn2_w3': ffn2_w3[l].astype(BF16), 'ffn2_w2': ffn2_w2[l].astype(BF16),
        'w_in': w_pad, 'w_out': w_out[l].astype(BF16),
        'short_conv_w': short_conv_w[l], 'ssd_conv_w': ssd_conv_w[l], 'ssd_conv_b': row(ssd_conv_b),
        'ssd_dt_bias': pad_heads(ssd_dt_bias), 'ssd_a_log': pad_heads(ssd_a_log),
        'ssd_d': jnp.repeat(ssd_d[l], SSD_HEAD_DIM).reshape(1, SSD_INNER), 'ssd_norm_g': row(ssd_norm_g),
    }


def kernel(x_prompt, x_sample, cache_sb_k, cache_sb_v, state_short_conv, state_ssd_conv, state_ssd,
           ln1_g, ln1_b, ffn1_w1, ffn1_w3, ffn1_w2, w_in, short_conv_w, ssd_conv_w, ssd_conv_b,
           ssd_dt_bias, ssd_a_log, ssd_d, ssd_norm_g, w_out, ln2_g, ln2_b,
           ffn2_w1, ffn2_w3, ffn2_w2, ln3_g, ln3_b):
    bp, lp_, _ = x_prompt.shape
    bs, ls, _ = x_sample.shape
    past_len = cache_sb_k.shape[2]
    conv0 = jnp.zeros((bp, SHORT_CONV_W - 1, CONV_DIM), F32)
    ssd_conv0 = jnp.zeros((bp, SSD_CONV_W - 1, SSD_CONV_DIM), F32)
    ssd0 = jnp.zeros((bp, SSD_HEADS, SSD_HEAD_DIM, SSD_STATE), F32)
    xp = x_prompt.reshape(bp * lp_, D_MODEL)
    xs = x_sample.reshape(bs * ls, D_MODEL)
    sp, ss = [], []
    for l in range(DEPTH):
        lp = _layer_params(l, ln1_g, ln1_b, ffn1_w1, ffn1_w3, ffn1_w2, w_in, short_conv_w, ssd_conv_w,
                           ssd_conv_b, ssd_dt_bias, ssd_a_log, ssd_d, ssd_norm_g, w_out, ln2_g, ln2_b,
                           ffn2_w1, ffn2_w3, ffn2_w2, ln3_g, ln3_b)
        xp, st_p = _layer(xp, bp, lp_, lp, conv0, ssd_conv0, ssd0, PROMPT_CHUNK, PROMPT_Q_TILE, None, None)
        xs, st_s = _layer(xs, bs, ls, lp, state_short_conv[l], state_ssd_conv[l], state_ssd[l], ls, ls,
                          cache_sb_k[l].reshape(bs, past_len, SB_DIM), cache_sb_v[l].reshape(bs, past_len, SB_DIM))
        sp.append(st_p)
        ss.append(st_s)
    outs = [xp.reshape(bp, lp_, D_MODEL), xs.reshape(bs, ls, D_MODEL)]
    for group in (sp, ss):
        for j in range(5):
            outs.append(jnp.stack([s[j] for s in group]))
    return tuple(outs)
```

```python
import functools

import jax
import jax.numpy as jnp
from jax import lax
from jax.experimental import pallas as pl
from jax.experimental.pallas import tpu as pltpu

F32 = jnp.float32
BF16 = jnp.bfloat16

D_MODEL = 1024
DEPTH = 4
D_FF = 2816
CONV_DIM = 256
SHORT_CONV_W = 3
SB_HEADS = 4
HEAD_DIM = 64
SB_DIM = SB_HEADS * HEAD_DIM
SSD_HEADS = 8
SSD_HEAD_DIM = 64
SSD_INNER = SSD_HEADS * SSD_HEAD_DIM
SSD_GROUPS = 2
SSD_STATE = 128
SSD_CONV_W = 4
SSD_CONV_DIM = SSD_INNER + 2 * SSD_GROUPS * SSD_STATE
D_MIX = CONV_DIM + SB_DIM + SSD_INNER
D_IN_PROJ = 3 * CONV_DIM + 3 * SB_DIM + SSD_INNER + SSD_CONV_DIM + SSD_HEADS
ALPHA = (2 * DEPTH) ** 0.25
LN_EPS = 1e-5
RMS_EPS = 1e-5

LANES = 128
SUBLANES = 8
HALO = SUBLANES
DT_PAD = LANES
D_IN_PAD = D_IN_PROJ - SSD_HEADS + DT_PAD
TM = 512
TF = 256
KEY_BLOCK = 256
PROMPT_Q_TILE = 256
PROMPT_CHUNK = 128
EXP_UNDERFLOW = -104.0
VMEM_LIMIT = 56 * 1024 * 1024

_O_BCU = 0
_O_Q = 3 * CONV_DIM
_O_K = _O_Q + SB_DIM
_O_V = _O_K + SB_DIM
_O_Z = _O_V + SB_DIM
_O_XBC = _O_Z + SSD_INNER
_O_DT = _O_XBC + SSD_CONV_DIM


def _sigmoid(x):
    return 0.5 * jnp.tanh(0.5 * x) + 0.5


def _silu(x):
    return x * _sigmoid(x)


def _softplus(x):
    return jnp.maximum(x, 0.0) + jnp.log1p(jnp.exp(-jnp.abs(x)))


def _log_sigmoid_neg(x):
    return jnp.minimum(-x, 0.0) - jnp.log(1.0 + jnp.exp(-jnp.abs(x)))


def _layer_norm(y, g, b):
    mu = jnp.mean(y, axis=-1, keepdims=True)
    d = y - mu
    var = jnp.mean(d * d, axis=-1, keepdims=True)
    return d * lax.rsqrt(var + LN_EPS) * g + b


def _split3(x):
    hi = x.astype(BF16)
    r = x - hi.astype(F32)
    mid = r.astype(BF16)
    lo = (r - mid.astype(F32)).astype(BF16)
    return hi, mid, lo


def _dot(a, b):
    return jnp.dot(a, b, preferred_element_type=F32)


def _dot_nt(a, b):
    return lax.dot_general(a, b, (((1,), (1,)), ((), ())), preferred_element_type=F32)


def _dot_tn(a, b):
    return lax.dot_general(a, b, (((0,), (0,)), ((), ())), preferred_element_type=F32)


def _ffn_block(x, w1_ref, w3_ref, w2_ref, g_ref, b_ref):
    xb = x.astype(BF16)
    acc = None
    for c in range(D_FF // TF):
        cols = slice(c * TF, (c + 1) * TF)
        h1 = _dot(xb, w1_ref[:, cols])
        h3 = _dot(xb, w3_ref[:, cols])
        part = _dot((_silu(h1) * h3).astype(BF16), w2_ref[cols, :])
        acc = part if acc is None else acc + part
    return _layer_norm(ALPHA * x + 0.5 * acc, g_ref[...], b_ref[...])


def _ffn_kernel(x_ref, w1_ref, w3_ref, w2_ref, g_ref, b_ref, o_ref):
    o_ref[...] = _ffn_block(x_ref[...], w1_ref, w3_ref, w2_ref, g_ref, b_ref)


def _mix_ffn_kernel(x_ref, yc_ref, ysb_ref, yssd_ref, wo_ref, g2_ref, b2_ref,
                    w1_ref, w3_ref, w2_ref, g3_ref, b3_ref, o_ref):
    mix_in = jnp.concatenate([yc_ref[...], ysb_ref[...], yssd_ref[...]], axis=-1).astype(BF16)
    x = _layer_norm(ALPHA * x_ref[...] + _dot(mix_in, wo_ref[...]), g2_ref[...], b2_ref[...])
    o_ref[...] = _ffn_block(x, w1_ref, w3_ref, w2_ref, g3_ref, b3_ref)


def _layer_slab(l, *dims):
    return pl.BlockSpec((None,) + dims, lambda *_: (l,) + (0,) * len(dims), pipeline_mode=pl.Buffered(1))


def _tokens(width):
    return pl.BlockSpec((TM, width), lambda i: (i, 0))


def _ffn_weight_specs(l):
    return [_layer_slab(l, D_MODEL, D_FF), _layer_slab(l, D_MODEL, D_FF), _layer_slab(l, D_FF, D_MODEL),
            _layer_slab(l, 1, D_MODEL), _layer_slab(l, 1, D_MODEL)]


_DENSE_PARAMS = pltpu.CompilerParams(dimension_semantics=("arbitrary",), vmem_limit_bytes=VMEM_LIMIT)


def _ffn(x, l, w1, w3, w2, g, b):
    m = x.shape[0]
    return pl.pallas_call(
        _ffn_kernel,
        grid=(m // TM,),
        in_specs=[_tokens(D_MODEL)] + _ffn_weight_specs(l),
        out_specs=_tokens(D_MODEL),
        out_shape=jax.ShapeDtypeStruct((m, D_MODEL), F32),
        compiler_params=_DENSE_PARAMS,
        name="ffn",
    )(x, w1, w3, w2, g, b)


def _mix_ffn(x, yc, ysb, yssd, l, wo, g2, b2, w1, w3, w2, g3, b3):
    m = x.shape[0]
    return pl.pallas_call(
        _mix_ffn_kernel,
        grid=(m // TM,),
        in_specs=[_tokens(D_MODEL), _tokens(CONV_DIM), _tokens(SB_DIM), _tokens(SSD_INNER),
                  _layer_slab(l, D_MIX, D_MODEL), _layer_slab(l, 1, D_MODEL), _layer_slab(l, 1, D_MODEL)]
                 + _ffn_weight_specs(l),
        out_specs=_tokens(D_MODEL),
        out_shape=jax.ShapeDtypeStruct((m, D_MODEL), F32),
        compiler_params=_DENSE_PARAMS,
        name="mix_ffn",
    )(x, yc, ysb, yssd, wo, g2, b2, w1, w3, w2, g3, b3)


def _inproj_kernel(x_ref, w_ref, k_stack_ref, v_stack_ref,
                   bcu_ref, q_ref, k_ref, v_ref, kb_ref, vb_ref, z_ref, xbc_ref, dt_ref):
    del k_stack_ref, v_stack_ref
    xb = x_ref[...].astype(BF16)

    def seg(lo, hi):
        return _dot(xb, w_ref[:, lo:hi])

    bcu_ref[...] = seg(_O_BCU, _O_Q)
    q_ref[...] = (seg(_O_Q, _O_K) * (HEAD_DIM ** -0.5)).astype(BF16)
    k = seg(_O_K, _O_V)
    k_ref[...] = k
    kb_ref[...] = k.astype(BF16)
    v = seg(_O_V, _O_Z)
    v_ref[...] = v
    vb_ref[...] = v.astype(BF16)
    z_ref[...] = seg(_O_Z, _O_XBC)
    xbc_ref[...] = seg(_O_XBC, _O_DT)
    dt_ref[...] = seg(_O_DT, D_IN_PAD)


def _inproj(x, l, w, k_stack, v_stack):
    m = x.shape[0]
    stack = jax.ShapeDtypeStruct((DEPTH, m, SB_DIM), F32)
    outs = [(3 * CONV_DIM, F32), (SB_DIM, BF16), None, None, (SB_DIM, BF16),
            (SB_DIM, BF16), (SSD_INNER, F32), (SSD_CONV_DIM, F32), (DT_PAD, F32)]
    layer_rows = pl.BlockSpec((None, TM, SB_DIM), lambda i: (l, i, 0))
    return pl.pallas_call(
        _inproj_kernel,
        grid=(m // TM,),
        in_specs=[_tokens(D_MODEL), _layer_slab(l, D_MODEL, D_IN_PAD),
                  pl.BlockSpec(memory_space=pl.ANY), pl.BlockSpec(memory_space=pl.ANY)],
        out_specs=[layer_rows if o is None else _tokens(o[0]) for o in outs],
        out_shape=[stack if o is None else jax.ShapeDtypeStruct((m, o[0]), o[1]) for o in outs],
        input_output_aliases={2: 2, 3: 3},
        compiler_params=_DENSE_PARAMS,
        name="inproj",
    )(x, w, k_stack, v_stack)


def _mixer_kernel(bcu_ref, z_ref, xbc_ref, dt_ref, sc0_ref, xc0_ref, h0_ref,
                  scw_ref, xcw_ref, xcb_ref, dtb_ref, alog_ref, dvec_ref, ng_ref, ltri_ref,
                  yconv_ref, yssd_ref, sctail_ref, hout_ref,
                  ext_s, ext_x, h_scr, *, chunk):
    c = pl.program_id(1)

    @pl.when(c == 0)
    def _():
        ext_s[0:HALO, :] = sc0_ref[0]
        ext_x[0:HALO, :] = xc0_ref[0]
        h_scr[...] = h0_ref[0]

    bcu = bcu_ref[0]
    b_gate = bcu[:, 0:CONV_DIM]
    cu = bcu[:, CONV_DIM:2 * CONV_DIM] * bcu[:, 2 * CONV_DIM:3 * CONV_DIM]
    ext_s[HALO:HALO + chunk, :] = cu
    conv = None
    for t in range(SHORT_CONV_W):
        term = ext_s[pl.ds(HALO - (SHORT_CONV_W - 1) + t, chunk), :] * scw_ref[t:t + 1, :]
        conv = term if conv is None else conv + term
    yconv_ref[0] = b_gate * conv
    tail_s = ext_s[chunk:chunk + HALO, :]
    ext_s[0:HALO, :] = tail_s
    sctail_ref[0] = tail_s

    ext_x[HALO:HALO + chunk, :] = xbc_ref[0]
    xc = None
    for t in range(SSD_CONV_W):
        term = ext_x[pl.ds(HALO - (SSD_CONV_W - 1) + t, chunk), :] * xcw_ref[t:t + 1, :]
        xc = term if xc is None else xc + term
    xc = xc + xcb_ref[...]
    xc = _silu(xc)
    ext_x[0:HALO, :] = ext_x[chunk:chunk + HALO, :]

    xs = xc[:, 0:SSD_INNER]
    b_all = xc[:, SSD_INNER:SSD_INNER + SSD_GROUPS * SSD_STATE]
    c_all = xc[:, SSD_INNER + SSD_GROUPS * SSD_STATE:SSD_CONV_DIM]

    dt = _softplus(dt_ref[0] + dtb_ref[...])
    a = -jnp.exp(alog_ref[...])
    da_hi, da_mid, da_lo = _split3(dt * a)
    ltri = ltri_ref[...]
    a_cum = _dot(ltri, da_hi) + _dot(ltri, da_mid) + _dot(ltri, da_lo)
    a_cum_t = a_cum.T
    dt_t = dt.T

    row = lax.broadcasted_iota(jnp.int32, (chunk, chunk), 0)
    col = lax.broadcasted_iota(jnp.int32, (chunk, chunk), 1)
    causal = row >= col
    heads_per_group = SSD_HEADS // SSD_GROUPS

    ys = []
    for g in range(SSD_GROUPS):
        bg = b_all[:, g * SSD_STATE:(g + 1) * SSD_STATE]
        cg = c_all[:, g * SSD_STATE:(g + 1) * SSD_STATE]
        bgb = bg.astype(BF16)
        gram = _dot_nt(cg.astype(BF16), bgb)
        for hh in range(heads_per_group):
            h = g * heads_per_group + hh
            acol = a_cum[:, h:h + 1]
            arow = a_cum_t[h:h + 1, :]
            dtrow = dt_t[h:h + 1, :]
            dtcol = dt[:, h:h + 1]
            seg = jnp.where(causal, acol - arow, 0.0)
            decay = jnp.where(causal, jnp.exp(seg), 0.0)
            scores = (gram * decay * dtrow).astype(BF16)
            xh = xs[:, h * SSD_HEAD_DIM:(h + 1) * SSD_HEAD_DIM]
            y = _dot(scores, xh.astype(BF16))
            h_prev = h_scr[h]
            cw = (cg * jnp.exp(acol)).astype(BF16)
            y = y + _dot_nt(cw, h_prev.astype(BF16))
            alast = a_cum[chunk - 1:chunk, h:h + 1]
            xw = (xh * (jnp.exp(alast - acol) * dtcol)).astype(BF16)
            h_scr[h] = jnp.exp(alast) * h_prev + _dot_tn(xw, bgb)
            y = y + dvec_ref[:, h * SSD_HEAD_DIM:(h + 1) * SSD_HEAD_DIM] * xh
            ys.append(y)
    y_all = jnp.concatenate(ys, axis=-1)
    zz = z_ref[0]
    y_all = y_all * _silu(zz)
    ms = jnp.mean(y_all * y_all, axis=-1, keepdims=True)
    yssd_ref[0] = y_all * lax.rsqrt(ms + RMS_EPS) * ng_ref[...]
    hout_ref[0] = h_scr[...]


def _mixer(bcu, z, xbc, dt, sc0, xc0, h0, scw, xcw, xcb, dtb, alog, dvec, ng, chunk):
    bsz, length, _ = bcu.shape
    ltri = jnp.tril(jnp.ones((chunk, chunk), F32)).astype(BF16)

    def tok(width):
        return pl.BlockSpec((1, chunk, width), lambda b, c: (b, c, 0))

    def per_batch(*dims):
        return pl.BlockSpec((1,) + dims, lambda b, c: (b,) + (0,) * len(dims))

    def whole(arr):
        return pl.BlockSpec(arr.shape, lambda b, c: (0,) * arr.ndim)

    return pl.pallas_call(
        functools.partial(_mixer_kernel, chunk=chunk),
        grid=(bsz, length // chunk),
        in_specs=[tok(3 * CONV_DIM), tok(SSD_INNER), tok(SSD_CONV_DIM), tok(DT_PAD),
                  per_batch(HALO, CONV_DIM), per_batch(HALO, SSD_CONV_DIM),
                  per_batch(SSD_HEADS, SSD_HEAD_DIM, SSD_STATE),
                  whole(scw), whole(xcw), whole(xcb), whole(dtb), whole(alog), whole(dvec), whole(ng),
                  whole(ltri)],
        out_specs=[tok(CONV_DIM), tok(SSD_INNER), per_batch(HALO, CONV_DIM),
                   per_batch(SSD_HEADS, SSD_HEAD_DIM, SSD_STATE)],
        out_shape=[jax.ShapeDtypeStruct((bsz, length, CONV_DIM), F32),
                   jax.ShapeDtypeStruct((bsz, length, SSD_INNER), F32),
                   jax.ShapeDtypeStruct((bsz, HALO, CONV_DIM), F32),
                   jax.ShapeDtypeStruct((bsz, SSD_HEADS, SSD_HEAD_DIM, SSD_STATE), F32)],
        scratch_shapes=[pltpu.VMEM((HALO + chunk, CONV_DIM), F32),
                        pltpu.VMEM((HALO + chunk, SSD_CONV_DIM), F32),
                        pltpu.VMEM((SSD_HEADS, SSD_HEAD_DIM, SSD_STATE), F32)],
        compiler_params=pltpu.CompilerParams(
            dimension_semantics=("arbitrary", "arbitrary"), vmem_limit_bytes=VMEM_LIMIT),
        name="mixer",
    )(bcu, z, xbc, dt, sc0, xc0, h0, scw, xcw, xcb, dtb, alog, dvec, ng, ltri)


def _attn_kernel(q_ref, kd_ref, vd_ref, kp_ref, vp_ref, rd_ref, rp_ref,
                 o_ref, carry_ref, acc_ref, *, tq, past_is_prefix):
    i = pl.program_id(1)
    n_pairs = SB_DIM // LANES
    q = q_ref[0]
    lane = lax.broadcasted_iota(jnp.int32, (1, LANES), 1)
    low_head = lane < HEAD_DIM
    q_stack = []
    for p in range(n_pairs):
        qp = q[:, p * LANES:(p + 1) * LANES]
        zero = jnp.zeros_like(qp)
        q_stack.append(jnp.concatenate([jnp.where(low_head, qp, zero), jnp.where(low_head, zero, qp)], axis=0))

    def visit(kblk, vblk, r_ref, mask, first):
        nk = kblk.shape[0]
        kblk = kblk.astype(BF16)
        vblk = vblk.astype(BF16)
        z = jnp.concatenate([_dot_nt(q_stack[p], kblk[:, p * LANES:(p + 1) * LANES])
                             for p in range(n_pairs)], axis=0)
        lg = _log_sigmoid_neg(z)
        if mask is not None:
            lg = jnp.where(mask, lg, 0.0)
        hi = lg.astype(BF16)
        lo = (lg - hi.astype(F32)).astype(BF16)
        r = r_ref[...]
        res = _dot(hi, r) + _dot(lo, r)
        tot = res[:, 0:LANES]
        cs = res[:, LANES:LANES + nk]
        if not first:
            carry = carry_ref[...]
            tot = tot + carry
            cs = cs + (carry[:, 0:nk] if nk <= LANES else jnp.concatenate([carry] * (nk // LANES), axis=1))
        w = jnp.exp(z + cs)
        if mask is not None:
            w = jnp.where(mask, w, 0.0)
        wb = w.astype(BF16)
        outs = []
        for p in range(n_pairs):
            pv = _dot(wb[2 * p * tq:(2 * p + 2) * tq], vblk[:, p * LANES:(p + 1) * LANES])
            outs.append(jnp.where(low_head, pv[0:tq], pv[tq:2 * tq]))
        out = jnp.concatenate(outs, axis=1)
        acc_ref[...] = out if first else acc_ref[...] + out
        carry_ref[...] = tot
        return jnp.max(tot)

    row = lax.broadcasted_iota(jnp.int32, (SB_HEADS * tq, tq), 0) & (tq - 1)
    col = lax.broadcasted_iota(jnp.int32, (SB_HEADS * tq, tq), 1)
    worst = visit(kd_ref[0], vd_ref[0], rd_ref, col < row, True)

    n_past = i * (tq // KEY_BLOCK) if past_is_prefix else kp_ref.shape[1] // KEY_BLOCK

    def cond(state):
        kb, worst_carry = state
        return jnp.logical_and(kb >= 0, worst_carry >= EXP_UNDERFLOW)

    def body(state):
        kb, _ = state
        start = pl.multiple_of(kb * KEY_BLOCK, KEY_BLOCK)
        return kb - 1, visit(kp_ref[0, pl.ds(start, KEY_BLOCK), :], vp_ref[0, pl.ds(start, KEY_BLOCK), :],
                             rp_ref, None, False)

    lax.while_loop(cond, body, (n_past - 1, worst))
    o_ref[0] = acc_ref[...]


def _suffix_sum_rhs(n):
    return jnp.concatenate([jnp.ones((n, LANES), F32), jnp.tril(jnp.ones((n, n), F32))], axis=1).astype(BF16)


def _attn(q, k_new, v_new, k_past, v_past, tq, past_is_prefix):
    bsz, length, _ = q.shape
    past_len = k_past.shape[1]
    assert tq & (tq - 1) == 0 and past_len % KEY_BLOCK == 0
    assert not past_is_prefix or tq % KEY_BLOCK == 0
    rd = _suffix_sum_rhs(tq)
    rp = _suffix_sum_rhs(KEY_BLOCK)

    def tile():
        return pl.BlockSpec((1, tq, SB_DIM), lambda b, i: (b, i, 0))

    def past():
        return pl.BlockSpec((1, past_len, SB_DIM), lambda b, i: (b, 0, 0))

    def whole(arr):
        return pl.BlockSpec(arr.shape, lambda b, i: (0,) * arr.ndim)

    return pl.pallas_call(
        functools.partial(_attn_kernel, tq=tq, past_is_prefix=past_is_prefix),
        grid=(bsz, length // tq),
        in_specs=[tile(), tile(), tile(), past(), past(), whole(rd), whole(rp)],
        out_specs=tile(),
        out_shape=jax.ShapeDtypeStruct((bsz, length, SB_DIM), F32),
        scratch_shapes=[pltpu.VMEM((SB_HEADS * tq, LANES), F32), pltpu.VMEM((tq, SB_DIM), F32)],
        compiler_params=pltpu.CompilerParams(
            dimension_semantics=("arbitrary", "arbitrary"), vmem_limit_bytes=VMEM_LIMIT),
        name="attn",
    )(q, k_new, v_new, k_past, v_past, rd, rp)


def _pad_rows_front(state):
    return jnp.pad(state, ((0, 0), (HALO - state.shape[1], 0), (0, 0)))


def _layer(x, bsz, length, l, dw, lp, k_stack, v_stack, conv_prev, ssd_conv_prev, ssd_h0, chunk, tq,
           k_past, v_past):
    x = _ffn(x, l, dw['ffn1_w1'], dw['ffn1_w3'], dw['ffn1_w2'], dw['ln1_g'], dw['ln1_b'])
    bcu, q, k_stack, v_stack, kb, vb, z, xbc, dt = _inproj(x, l, dw['w_in'], k_stack, v_stack)

    def r3(a):
        return a.reshape(bsz, length, a.shape[-1])

    xbc3 = r3(xbc)
    yconv, yssd, sctail, h_new = _mixer(
        r3(bcu), r3(z), xbc3, r3(dt), _pad_rows_front(conv_prev), _pad_rows_front(ssd_conv_prev), ssd_h0,
        lp['short_conv_w'], lp['ssd_conv_w'], lp['ssd_conv_b'], lp['ssd_dt_bias'], lp['ssd_a_log'],
        lp['ssd_d'], lp['ssd_norm_g'], chunk)
    if k_past is None:
        ysb = _attn(r3(q), r3(kb), r3(vb), r3(kb), r3(vb), tq, True)
    else:
        ysb = _attn(r3(q), r3(kb), r3(vb), k_past, v_past, tq, False)
    x = _mix_ffn(x, yconv.reshape(-1, CONV_DIM), ysb.reshape(-1, SB_DIM), yssd.reshape(-1, SSD_INNER), l,
                 dw['w_out'], dw['ln2_g'], dw['ln2_b'],
                 dw['ffn2_w1'], dw['ffn2_w3'], dw['ffn2_w2'], dw['ln3_g'], dw['ln3_b'])
    states = (sctail[:, HALO - (SHORT_CONV_W - 1):, :], xbc3[:, length - (SSD_CONV_W - 1):, :], h_new)
    return x, k_stack, v_stack, states


def _mixer_params(l, short_conv_w, ssd_conv_w, ssd_conv_b, ssd_dt_bias, ssd_a_log, ssd_d, ssd_norm_g):
    def row(a):
        return a[l].reshape(1, -1)

    def pad_heads(a):
        return jnp.pad(a[l], (0, DT_PAD - SSD_HEADS)).reshape(1, DT_PAD)

    return {
        'short_conv_w': short_conv_w[l], 'ssd_conv_w': ssd_conv_w[l], 'ssd_conv_b': row(ssd_conv_b),
        'ssd_dt_bias': pad_heads(ssd_dt_bias), 'ssd_a_log': pad_heads(ssd_a_log),
        'ssd_d': jnp.repeat(ssd_d[l], SSD_HEAD_DIM).reshape(1, SSD_INNER), 'ssd_norm_g': row(ssd_norm_g),
    }


def _dense_params(w_in, **stacks):
    out = {}
    for name, a in stacks.items():
        out[name] = a.astype(BF16) if a.ndim == 3 else a.reshape(DEPTH, 1, D_MODEL)
    out['w_in'] = jnp.pad(w_in, ((0, 0), (0, 0), (0, D_IN_PAD - D_IN_PROJ))).astype(BF16)
    return out


def kernel(x_prompt, x_sample, cache_sb_k, cache_sb_v, state_short_conv, state_ssd_conv, state_ssd,
           ln1_g, ln1_b, ffn1_w1, ffn1_w3, ffn1_w2, w_in, short_conv_w, ssd_conv_w, ssd_conv_b,
           ssd_dt_bias, ssd_a_log, ssd_d, ssd_norm_g, w_out, ln2_g, ln2_b,
           ffn2_w1, ffn2_w3, ffn2_w2, ln3_g, ln3_b):
    bp, lp_, _ = x_prompt.shape
    bs, ls, _ = x_sample.shape
    past_len = cache_sb_k.shape[2]
    conv0 = jnp.zeros((bp, SHORT_CONV_W - 1, CONV_DIM), F32)
    ssd_conv0 = jnp.zeros((bp, SSD_CONV_W - 1, SSD_CONV_DIM), F32)
    ssd0 = jnp.zeros((bp, SSD_HEADS, SSD_HEAD_DIM, SSD_STATE), F32)
    xp = x_prompt.reshape(bp * lp_, D_MODEL)
    xs = x_sample.reshape(bs * ls, D_MODEL)
    dw = _dense_params(w_in, ffn1_w1=ffn1_w1, ffn1_w3=ffn1_w3, ffn1_w2=ffn1_w2, w_out=w_out,
                       ffn2_w1=ffn2_w1, ffn2_w3=ffn2_w3, ffn2_w2=ffn2_w2, ln1_g=ln1_g, ln1_b=ln1_b,
                       ln2_g=ln2_g, ln2_b=ln2_b, ln3_g=ln3_g, ln3_b=ln3_b)
    kp_stack = jnp.zeros((DEPTH, bp * lp_, SB_DIM), F32)
    vp_stack = jnp.zeros((DEPTH, bp * lp_, SB_DIM), F32)
    ks_stack = jnp.zeros((DEPTH, bs * ls, SB_DIM), F32)
    vs_stack = jnp.zeros((DEPTH, bs * ls, SB_DIM), F32)
    sp, ss = [], []
    for l in range(DEPTH):
        lp = _mixer_params(l, short_conv_w, ssd_conv_w, ssd_conv_b, ssd_dt_bias, ssd_a_log, ssd_d, ssd_norm_g)
        xp, kp_stack, vp_stack, st_p = _layer(xp, bp, lp_, l, dw, lp, kp_stack, vp_stack, conv0, ssd_conv0, ssd0,
                                              PROMPT_CHUNK, PROMPT_Q_TILE, None, None)
        xs, ks_stack, vs_stack, st_s = _layer(
            xs, bs, ls, l, dw, lp, ks_stack, vs_stack, state_short_conv[l], state_ssd_conv[l], state_ssd[l], ls, ls,
            cache_sb_k[l].reshape(bs, past_len, SB_DIM), cache_sb_v[l].reshape(bs, past_len, SB_DIM))
        sp.append(st_p)
        ss.append(st_s)
    outs = [xp.reshape(bp, lp_, D_MODEL), xs.reshape(bs, ls, D_MODEL)]
    for (k_stack, v_stack, group, bsz, length) in ((kp_stack, vp_stack, sp, bp, lp_), (ks_stack, vs_stack, ss, bs, ls)):
        outs.append(k_stack.reshape(DEPTH, bsz, length, SB_HEADS, HEAD_DIM))
        outs.append(v_stack.reshape(DEPTH, bsz, length, SB_HEADS, HEAD_DIM))
        for j in range(3):
            outs.append(jnp.stack([s[j] for s in group]))
    return tuple(outs)
```

```python
import functools

import jax
import jax.numpy as jnp
from jax import lax
from jax.experimental import pallas as pl
from jax.experimental.pallas import tpu as pltpu

F32 = jnp.float32
BF16 = jnp.bfloat16

D_MODEL = 1024
DEPTH = 4
D_FF = 2816
CONV_DIM = 256
SHORT_CONV_W = 3
SB_HEADS = 4
HEAD_DIM = 64
SB_DIM = SB_HEADS * HEAD_DIM
SSD_HEADS = 8
SSD_HEAD_DIM = 64
SSD_INNER = SSD_HEADS * SSD_HEAD_DIM
SSD_GROUPS = 2
SSD_STATE = 128
SSD_CONV_W = 4
SSD_CONV_DIM = SSD_INNER + 2 * SSD_GROUPS * SSD_STATE
D_MIX = CONV_DIM + SB_DIM + SSD_INNER
D_IN_PROJ = 3 * CONV_DIM + 3 * SB_DIM + SSD_INNER + SSD_CONV_DIM + SSD_HEADS
ALPHA = (2 * DEPTH) ** 0.25
LN_EPS = 1e-5
RMS_EPS = 1e-5

LANES = 128
SUBLANES = 8
HALO = SUBLANES
DT_PAD = LANES
D_IN_PAD = D_IN_PROJ - SSD_HEADS + DT_PAD
TM = 1024
TM_IN = 512
TF = 256
KEY_BLOCK = 256
PROMPT_Q_TILE = 256
PROMPT_CHUNK = 128
EXP_UNDERFLOW = -104.0
VMEM_LIMIT = 56 * 1024 * 1024

_O_BCU = 0
_O_Q = 3 * CONV_DIM
_O_K = _O_Q + SB_DIM
_O_V = _O_K + SB_DIM
_O_Z = _O_V + SB_DIM
_O_XBC = _O_Z + SSD_INNER
_O_DT = _O_XBC + SSD_CONV_DIM


def _sigmoid(x):
    return 0.5 * jnp.tanh(0.5 * x) + 0.5


def _silu(x):
    return x * _sigmoid(x)


def _softplus(x):
    return jnp.maximum(x, 0.0) + jnp.log1p(jnp.exp(-jnp.abs(x)))


def _log_sigmoid_neg(x):
    return jnp.minimum(-x, 0.0) - jnp.log(1.0 + jnp.exp(-jnp.abs(x)))


def _layer_norm(y, g, b):
    mu = jnp.mean(y, axis=-1, keepdims=True)
    d = y - mu
    var = jnp.mean(d * d, axis=-1, keepdims=True)
    return d * lax.rsqrt(var + LN_EPS) * g + b


def _split3(x):
    hi = x.astype(BF16)
    r = x - hi.astype(F32)
    mid = r.astype(BF16)
    lo = (r - mid.astype(F32)).astype(BF16)
    return hi, mid, lo


def _dot(a, b):
    return jnp.dot(a, b, preferred_element_type=F32)


def _dot_nt(a, b):
    return lax.dot_general(a, b, (((1,), (1,)), ((), ())), preferred_element_type=F32)


def _dot_tn(a, b):
    return lax.dot_general(a, b, (((0,), (0,)), ((), ())), preferred_element_type=F32)


def _ffn_block(x, w1_ref, w3_ref, w2_ref, g_ref, b_ref):
    xb = x.astype(BF16)
    acc = None
    for c in range(D_FF // TF):
        cols = slice(c * TF, (c + 1) * TF)
        h1 = _dot(xb, w1_ref[:, cols])
        h3 = _dot(xb, w3_ref[:, cols])
        part = _dot((_silu(h1) * h3).astype(BF16), w2_ref[cols, :])
        acc = part if acc is None else acc + part
    return _layer_norm(ALPHA * x + 0.5 * acc, g_ref[...], b_ref[...])


def _ffn_kernel(x_ref, w1_ref, w3_ref, w2_ref, g_ref, b_ref, o_ref):
    o_ref[...] = _ffn_block(x_ref[...], w1_ref, w3_ref, w2_ref, g_ref, b_ref)


def _mix_ffn_kernel(x_ref, yc_ref, ysb_ref, yssd_ref, wo_ref, g2_ref, b2_ref,
                    w1_ref, w3_ref, w2_ref, g3_ref, b3_ref, o_ref):
    mix_in = jnp.concatenate([yc_ref[...], ysb_ref[...], yssd_ref[...]], axis=-1).astype(BF16)
    x = _layer_norm(ALPHA * x_ref[...] + _dot(mix_in, wo_ref[...]), g2_ref[...], b2_ref[...])
    o_ref[...] = _ffn_block(x, w1_ref, w3_ref, w2_ref, g3_ref, b3_ref)


def _layer_slab(l, *dims):
    return pl.BlockSpec((None,) + dims, lambda *_: (l,) + (0,) * len(dims), pipeline_mode=pl.Buffered(1))


def _tokens(width):
    return pl.BlockSpec((TM, width), lambda i: (i, 0))


def _ffn_weight_specs(l):
    return [_layer_slab(l, D_MODEL, D_FF), _layer_slab(l, D_MODEL, D_FF), _layer_slab(l, D_FF, D_MODEL),
            _layer_slab(l, 1, D_MODEL), _layer_slab(l, 1, D_MODEL)]


_DENSE_PARAMS = pltpu.CompilerParams(dimension_semantics=("arbitrary",), vmem_limit_bytes=VMEM_LIMIT)


def _ffn(x, l, w1, w3, w2, g, b):
    m = x.shape[0]
    return pl.pallas_call(
        _ffn_kernel,
        grid=(m // TM,),
        in_specs=[_tokens(D_MODEL)] + _ffn_weight_specs(l),
        out_specs=_tokens(D_MODEL),
        out_shape=jax.ShapeDtypeStruct((m, D_MODEL), F32),
        compiler_params=_DENSE_PARAMS,
        name="ffn",
    )(x, w1, w3, w2, g, b)


def _mix_ffn(x, yc, ysb, yssd, l, wo, g2, b2, w1, w3, w2, g3, b3):
    m = x.shape[0]
    return pl.pallas_call(
        _mix_ffn_kernel,
        grid=(m // TM,),
        in_specs=[_tokens(D_MODEL), _tokens(CONV_DIM), _tokens(SB_DIM), _tokens(SSD_INNER),
                  _layer_slab(l, D_MIX, D_MODEL), _layer_slab(l, 1, D_MODEL), _layer_slab(l, 1, D_MODEL)]
                 + _ffn_weight_specs(l),
        out_specs=_tokens(D_MODEL),
        out_shape=jax.ShapeDtypeStruct((m, D_MODEL), F32),
        compiler_params=_DENSE_PARAMS,
        name="mix_ffn",
    )(x, yc, ysb, yssd, wo, g2, b2, w1, w3, w2, g3, b3)


def _inproj_kernel(x_ref, w_ref, k_stack_ref, v_stack_ref,
                   bcu_ref, q_ref, k_ref, v_ref, kb_ref, vb_ref, z_ref, xbc_ref, dt_ref, *, channel_major):
    del k_stack_ref, v_stack_ref
    xb = x_ref[...].astype(BF16)

    def seg(lo, hi):
        return _dot(xb, w_ref[:, lo:hi])

    bcu_ref[...] = seg(_O_BCU, _O_Q)
    q_ref[...] = (seg(_O_Q, _O_K) * (HEAD_DIM ** -0.5)).astype(BF16)
    k = seg(_O_K, _O_V)
    kb_ref[...] = k.astype(BF16)
    k_ref[...] = k.T if channel_major else k
    v = seg(_O_V, _O_Z)
    vb_ref[...] = v.astype(BF16)
    v_ref[...] = v.T if channel_major else v
    z_ref[...] = seg(_O_Z, _O_XBC)
    xbc_ref[...] = seg(_O_XBC, _O_DT)
    dt_ref[...] = seg(_O_DT, D_IN_PAD)


def _inproj(x, l, w, k_stack, v_stack):
    m = x.shape[0]
    channel_major = k_stack.ndim == 4
    stack = jax.ShapeDtypeStruct(k_stack.shape, F32)
    outs = [(3 * CONV_DIM, F32), (SB_DIM, BF16), None, None, (SB_DIM, BF16),
            (SB_DIM, BF16), (SSD_INNER, F32), (SSD_CONV_DIM, F32), (DT_PAD, F32)]
    if channel_major:
        tiles_per_seq = k_stack.shape[3] // TM_IN
        layer_rows = pl.BlockSpec((None, None, SB_DIM, TM_IN),
                                  lambda i: (l, i // tiles_per_seq, 0, i % tiles_per_seq))
    else:
        layer_rows = pl.BlockSpec((None, TM_IN, SB_DIM), lambda i: (l, i, 0))

    def _tokens(width):
        return pl.BlockSpec((TM_IN, width), lambda i: (i, 0))

    return pl.pallas_call(
        functools.partial(_inproj_kernel, channel_major=channel_major),
        grid=(m // TM_IN,),
        in_specs=[_tokens(D_MODEL), _layer_slab(l, D_MODEL, D_IN_PAD),
                  pl.BlockSpec(memory_space=pl.ANY), pl.BlockSpec(memory_space=pl.ANY)],
        out_specs=[layer_rows if o is None else _tokens(o[0]) for o in outs],
        out_shape=[stack if o is None else jax.ShapeDtypeStruct((m, o[0]), o[1]) for o in outs],
        input_output_aliases={2: 2, 3: 3},
        compiler_params=_DENSE_PARAMS,
        name="inproj",
    )(x, w, k_stack, v_stack)


def _mixer_kernel(bcu_ref, z_ref, xbc_ref, dt_ref, sc0_ref, xc0_ref, h0_ref,
                  scw_ref, xcw_ref, xcb_ref, dtb_ref, alog_ref, dvec_ref, ng_ref, ltri_ref,
                  yconv_ref, yssd_ref, sctail_ref, hout_ref,
                  ext_s, ext_x, h_scr, *, chunk):
    c = pl.program_id(1)

    @pl.when(c == 0)
    def _():
        ext_s[0:HALO, :] = sc0_ref[0]
        ext_x[0:HALO, :] = xc0_ref[0]
        h_scr[...] = h0_ref[0]

    bcu = bcu_ref[0]
    b_gate = bcu[:, 0:CONV_DIM]
    cu = bcu[:, CONV_DIM:2 * CONV_DIM] * bcu[:, 2 * CONV_DIM:3 * CONV_DIM]
    ext_s[HALO:HALO + chunk, :] = cu
    conv = None
    for t in range(SHORT_CONV_W):
        term = ext_s[pl.ds(HALO - (SHORT_CONV_W - 1) + t, chunk), :] * scw_ref[t:t + 1, :]
        conv = term if conv is None else conv + term
    yconv_ref[0] = b_gate * conv
    tail_s = ext_s[chunk:chunk + HALO, :]
    ext_s[0:HALO, :] = tail_s
    sctail_ref[0] = tail_s

    ext_x[HALO:HALO + chunk, :] = xbc_ref[0]
    xc = None
    for t in range(SSD_CONV_W):
        term = ext_x[pl.ds(HALO - (SSD_CONV_W - 1) + t, chunk), :] * xcw_ref[t:t + 1, :]
        xc = term if xc is None else xc + term
    xc = xc + xcb_ref[...]
    xc = _silu(xc)
    ext_x[0:HALO, :] = ext_x[chunk:chunk + HALO, :]

    xs = xc[:, 0:SSD_INNER]
    b_all = xc[:, SSD_INNER:SSD_INNER + SSD_GROUPS * SSD_STATE]
    c_all = xc[:, SSD_INNER + SSD_GROUPS * SSD_STATE:SSD_CONV_DIM]

    dt = _softplus(dt_ref[0] + dtb_ref[...])
    a = -jnp.exp(alog_ref[...])
    da_hi, da_mid, da_lo = _split3(dt * a)
    ltri = ltri_ref[...]
    a_cum = _dot(ltri, da_hi) + _dot(ltri, da_mid) + _dot(ltri, da_lo)
    a_cum_t = a_cum.T
    dt_t = dt.T

    row = lax.broadcasted_iota(jnp.int32, (chunk, chunk), 0)
    col = lax.broadcasted_iota(jnp.int32, (chunk, chunk), 1)
    causal = row >= col
    heads_per_group = SSD_HEADS // SSD_GROUPS

    ys = []
    for g in range(SSD_GROUPS):
        bg = b_all[:, g * SSD_STATE:(g + 1) * SSD_STATE]
        cg = c_all[:, g * SSD_STATE:(g + 1) * SSD_STATE]
        bgb = bg.astype(BF16)
        gram = _dot_nt(cg.astype(BF16), bgb)
        for hh in range(heads_per_group):
            h = g * heads_per_group + hh
            acol = a_cum[:, h:h + 1]
            arow = a_cum_t[h:h + 1, :]
            dtrow = dt_t[h:h + 1, :]
            dtcol = dt[:, h:h + 1]
            seg = jnp.where(causal, acol - arow, 0.0)
            decay = jnp.where(causal, jnp.exp(seg), 0.0)
            scores = (gram * decay * dtrow).astype(BF16)
            xh = xs[:, h * SSD_HEAD_DIM:(h + 1) * SSD_HEAD_DIM]
            y = _dot(scores, xh.astype(BF16))
            h_prev = h_scr[h]
            cw = (cg * jnp.exp(acol)).astype(BF16)
            y = y + _dot_nt(cw, h_prev.astype(BF16))
            alast = a_cum[chunk - 1:chunk, h:h + 1]
            xw = (xh * (jnp.exp(alast - acol) * dtcol)).astype(BF16)
            h_scr[h] = jnp.exp(alast) * h_prev + _dot_tn(xw, bgb)
            y = y + dvec_ref[:, h * SSD_HEAD_DIM:(h + 1) * SSD_HEAD_DIM] * xh
            ys.append(y)
    y_all = jnp.concatenate(ys, axis=-1)
    zz = z_ref[0]
    y_all = y_all * _silu(zz)
    ms = jnp.mean(y_all * y_all, axis=-1, keepdims=True)
    yssd_ref[0] = y_all * lax.rsqrt(ms + RMS_EPS) * ng_ref[...]
    hout_ref[0] = h_scr[...]


def _mixer(bcu, z, xbc, dt, sc0, xc0, h0, scw, xcw, xcb, dtb, alog, dvec, ng, chunk):
    bsz, length, _ = bcu.shape
    ltri = jnp.tril(jnp.ones((chunk, chunk), F32)).astype(BF16)

    def tok(width):
        return pl.BlockSpec((1, chunk, width), lambda b, c: (b, c, 0))

    def per_batch(*dims):
        return pl.BlockSpec((1,) + dims, lambda b, c: (b,) + (0,) * len(dims))

    def whole(arr):
        return pl.BlockSpec(arr.shape, lambda b, c: (0,) * arr.ndim)

    return pl.pallas_call(
        functools.partial(_mixer_kernel, chunk=chunk),
        grid=(bsz, length // chunk),
        in_specs=[tok(3 * CONV_DIM), tok(SSD_INNER), tok(SSD_CONV_DIM), tok(DT_PAD),
                  per_batch(HALO, CONV_DIM), per_batch(HALO, SSD_CONV_DIM),
                  per_batch(SSD_HEADS, SSD_HEAD_DIM, SSD_STATE),
                  whole(scw), whole(xcw), whole(xcb), whole(dtb), whole(alog), whole(dvec), whole(ng),
                  whole(ltri)],
        out_specs=[tok(CONV_DIM), tok(SSD_INNER), per_batch(HALO, CONV_DIM),
                   per_batch(SSD_HEADS, SSD_HEAD_DIM, SSD_STATE)],
        out_shape=[jax.ShapeDtypeStruct((bsz, length, CONV_DIM), F32),
                   jax.ShapeDtypeStruct((bsz, length, SSD_INNER), F32),
                   jax.ShapeDtypeStruct((bsz, HALO, CONV_DIM), F32),
                   jax.ShapeDtypeStruct((bsz, SSD_HEADS, SSD_HEAD_DIM, SSD_STATE), F32)],
        scratch_shapes=[pltpu.VMEM((HALO + chunk, CONV_DIM), F32),
                        pltpu.VMEM((HALO + chunk, SSD_CONV_DIM), F32),
                        pltpu.VMEM((SSD_HEADS, SSD_HEAD_DIM, SSD_STATE), F32)],
        compiler_params=pltpu.CompilerParams(
            dimension_semantics=("arbitrary", "arbitrary"), vmem_limit_bytes=VMEM_LIMIT),
        name="mixer",
    )(bcu, z, xbc, dt, sc0, xc0, h0, scw, xcw, xcb, dtb, alog, dvec, ng, ltri)


def _attn_kernel(q_ref, kd_ref, vd_ref, kp_ref, vp_ref, rd_ref, rp_ref,
                 o_ref, carry_ref, acc_ref, *, tq, past_is_prefix):
    i = pl.program_id(1)
    n_pairs = SB_DIM // LANES
    q = q_ref[0]
    lane = lax.broadcasted_iota(jnp.int32, (1, LANES), 1)
    low_head = lane < HEAD_DIM
    q_stack = []
    for p in range(n_pairs):
        qp = q[:, p * LANES:(p + 1) * LANES]
        zero = jnp.zeros_like(qp)
        q_stack.append(jnp.concatenate([jnp.where(low_head, qp, zero), jnp.where(low_head, zero, qp)], axis=0))

    def visit(kblk, vblk, r_ref, mask, first, channel_major=False):
        nk = kblk.shape[1] if channel_major else kblk.shape[0]
        kblk = kblk.astype(BF16)
        vblk = vblk.astype(BF16)
        if channel_major:
            z = [_dot(q_stack[p], kblk[p * LANES:(p + 1) * LANES, :]) for p in range(n_pairs)]
        else:
            z = [_dot_nt(q_stack[p], kblk[:, p * LANES:(p + 1) * LANES]) for p in range(n_pairs)]
        z = jnp.concatenate(z, axis=0)
        lg = _log_sigmoid_neg(z)
        if mask is not None:
            lg = jnp.where(mask, lg, 0.0)
        hi = lg.astype(BF16)
        lo = (lg - hi.astype(F32)).astype(BF16)
        r = r_ref[...]
        res = _dot(hi, r) + _dot(lo, r)
        tot = res[:, 0:LANES]
        cs = res[:, LANES:LANES + nk]
        if not first:
            carry = carry_ref[...]
            tot = tot + carry
            cs = cs + (carry[:, 0:nk] if nk <= LANES else jnp.concatenate([carry] * (nk // LANES), axis=1))
        w = jnp.exp(z + cs)
        if mask is not None:
            w = jnp.where(mask, w, 0.0)
        wb = w.astype(BF16)
        outs = []
        for p in range(n_pairs):
            wp = wb[2 * p * tq:(2 * p + 2) * tq]
            if channel_major:
                pv = _dot_nt(wp, vblk[p * LANES:(p + 1) * LANES, :])
            else:
                pv = _dot(wp, vblk[:, p * LANES:(p + 1) * LANES])
            outs.append(jnp.where(low_head, pv[0:tq], pv[tq:2 * tq]))
        out = jnp.concatenate(outs, axis=1)
        acc_ref[...] = out if first else acc_ref[...] + out
        carry_ref[...] = tot
        return jnp.max(tot)

    row = lax.broadcasted_iota(jnp.int32, (SB_HEADS * tq, tq), 0) & (tq - 1)
    col = lax.broadcasted_iota(jnp.int32, (SB_HEADS * tq, tq), 1)
    worst = visit(kd_ref[0], vd_ref[0], rd_ref, col < row, True)

    n_past = i * (tq // KEY_BLOCK) if past_is_prefix else kp_ref.shape[2] // KEY_BLOCK

    def cond(state):
        kb, worst_carry = state
        return jnp.logical_and(kb >= 0, worst_carry >= EXP_UNDERFLOW)

    def body(state):
        kb, _ = state
        start = pl.multiple_of(kb * KEY_BLOCK, KEY_BLOCK)
        if past_is_prefix:
            return kb - 1, visit(kp_ref[0, pl.ds(start, KEY_BLOCK), :], vp_ref[0, pl.ds(start, KEY_BLOCK), :],
                                 rp_ref, None, False)
        return kb - 1, visit(kp_ref[0, :, pl.ds(start, KEY_BLOCK)], vp_ref[0, :, pl.ds(start, KEY_BLOCK)],
                             rp_ref, None, False, channel_major=True)

    lax.while_loop(cond, body, (n_past - 1, worst))
    o_ref[0] = acc_ref[...]


def _suffix_sum_rhs(n):
    return jnp.concatenate([jnp.ones((n, LANES), F32), jnp.tril(jnp.ones((n, n), F32))], axis=1).astype(BF16)


def _attn(q, k_new, v_new, k_past, v_past, tq, past_is_prefix):
    bsz, length, _ = q.shape
    past_len = k_past.shape[1] if past_is_prefix else k_past.shape[2]
    assert tq & (tq - 1) == 0 and past_len % KEY_BLOCK == 0
    assert not past_is_prefix or tq % KEY_BLOCK == 0
    rd = _suffix_sum_rhs(tq)
    rp = _suffix_sum_rhs(KEY_BLOCK)

    def tile():
        return pl.BlockSpec((1, tq, SB_DIM), lambda b, i: (b, i, 0))

    def past():
        return pl.BlockSpec((1,) + k_past.shape[1:], lambda b, i: (b, 0, 0))

    def whole(arr):
        return pl.BlockSpec(arr.shape, lambda b, i: (0,) * arr.ndim)

    return pl.pallas_call(
        functools.partial(_attn_kernel, tq=tq, past_is_prefix=past_is_prefix),
        grid=(bsz, length // tq),
        in_specs=[tile(), tile(), tile(), past(), past(), whole(rd), whole(rp)],
        out_specs=tile(),
        out_shape=jax.ShapeDtypeStruct((bsz, length, SB_DIM), F32),
        scratch_shapes=[pltpu.VMEM((SB_HEADS * tq, LANES), F32), pltpu.VMEM((tq, SB_DIM), F32)],
        compiler_params=pltpu.CompilerParams(
            dimension_semantics=("arbitrary", "arbitrary"), vmem_limit_bytes=VMEM_LIMIT),
        name="attn",
    )(q, k_new, v_new, k_past, v_past, rd, rp)


def _channel_major(cache):
    bsz, past = cache.shape[:2]
    return jnp.transpose(cache, (0, 2, 3, 1)).reshape(bsz, SB_DIM, past)


def _pad_rows_front(state):
    return jnp.pad(state, ((0, 0), (HALO - state.shape[1], 0), (0, 0)))


def _layer(x, bsz, length, l, dw, lp, k_stack, v_stack, conv_prev, ssd_conv_prev, ssd_h0, chunk, tq,
           k_past, v_past):
    x = _ffn(x, l, dw['ffn1_w1'], dw['ffn1_w3'], dw['ffn1_w2'], dw['ln1_g'], dw['ln1_b'])
    bcu, q, k_stack, v_stack, kb, vb, z, xbc, dt = _inproj(x, l, dw['w_in'], k_stack, v_stack)

    def r3(a):
        return a.reshape(bsz, length, a.shape[-1])

    xbc3 = r3(xbc)
    yconv, yssd, sctail, h_new = _mixer(
        r3(bcu), r3(z), xbc3, r3(dt), _pad_rows_front(conv_prev), _pad_rows_front(ssd_conv_prev), ssd_h0,
        lp['short_conv_w'], lp['ssd_conv_w'], lp['ssd_conv_b'], lp['ssd_dt_bias'], lp['ssd_a_log'],
        lp['ssd_d'], lp['ssd_norm_g'], chunk)
    if k_past is None:
        ysb = _attn(r3(q), r3(kb), r3(vb), r3(kb), r3(vb), tq, True)
    else:
        ysb = _attn(r3(q), r3(kb), r3(vb), k_past, v_past, tq, False)
    x = _mix_ffn(x, yconv.reshape(-1, CONV_DIM), ysb.reshape(-1, SB_DIM), yssd.reshape(-1, SSD_INNER), l,
                 dw['w_out'], dw['ln2_g'], dw['ln2_b'],
                 dw['ffn2_w1'], dw['ffn2_w3'], dw['ffn2_w2'], dw['ln3_g'], dw['ln3_b'])
    states = (sctail[:, HALO - (SHORT_CONV_W - 1):, :], xbc3[:, length - (SSD_CONV_W - 1):, :], h_new)
    return x, k_stack, v_stack, states


def _mixer_params(l, short_conv_w, ssd_conv_w, ssd_conv_b, ssd_dt_bias, ssd_a_log, ssd_d, ssd_norm_g):
    def row(a):
        return a[l].reshape(1, -1)

    def pad_heads(a):
        return jnp.pad(a[l], (0, DT_PAD - SSD_HEADS)).reshape(1, DT_PAD)

    return {
        'short_conv_w': short_conv_w[l], 'ssd_conv_w': ssd_conv_w[l], 'ssd_conv_b': row(ssd_conv_b),
        'ssd_dt_bias': pad_heads(ssd_dt_bias), 'ssd_a_log': pad_heads(ssd_a_log),
        'ssd_d': jnp.repeat(ssd_d[l], SSD_HEAD_DIM).reshape(1, SSD_INNER), 'ssd_norm_g': row(ssd_norm_g),
    }


def _dense_params(w_in, **stacks):
    out = {}
    for name, a in stacks.items():
        out[name] = a.astype(BF16) if a.ndim == 3 else a.reshape(DEPTH, 1, D_MODEL)
    out['w_in'] = jnp.pad(w_in, ((0, 0), (0, 0), (0, D_IN_PAD - D_IN_PROJ))).astype(BF16)
    return out


def kernel(x_prompt, x_sample, cache_sb_k, cache_sb_v, state_short_conv, state_ssd_conv, state_ssd,
           ln1_g, ln1_b, ffn1_w1, ffn1_w3, ffn1_w2, w_in, short_conv_w, ssd_conv_w, ssd_conv_b,
           ssd_dt_bias, ssd_a_log, ssd_d, ssd_norm_g, w_out, ln2_g, ln2_b,
           ffn2_w1, ffn2_w3, ffn2_w2, ln3_g, ln3_b):
    bp, lp_, _ = x_prompt.shape
    bs, ls, _ = x_sample.shape
    past_len = cache_sb_k.shape[2]
    conv0 = jnp.zeros((bp, SHORT_CONV_W - 1, CONV_DIM), F32)
    ssd_conv0 = jnp.zeros((bp, SSD_CONV_W - 1, SSD_CONV_DIM), F32)
    ssd0 = jnp.zeros((bp, SSD_HEADS, SSD_HEAD_DIM, SSD_STATE), F32)
    xp = x_prompt.reshape(bp * lp_, D_MODEL)
    xs = x_sample.reshape(bs * ls, D_MODEL)
    dw = _dense_params(w_in, ffn1_w1=ffn1_w1, ffn1_w3=ffn1_w3, ffn1_w2=ffn1_w2, w_out=w_out,
                       ffn2_w1=ffn2_w1, ffn2_w3=ffn2_w3, ffn2_w2=ffn2_w2, ln1_g=ln1_g, ln1_b=ln1_b,
                       ln2_g=ln2_g, ln2_b=ln2_b, ln3_g=ln3_g, ln3_b=ln3_b)
    kp_stack = jnp.zeros((DEPTH, bp, SB_DIM, lp_), F32)
    vp_stack = jnp.zeros((DEPTH, bp, SB_DIM, lp_), F32)
    ks_stack = jnp.zeros((DEPTH, bs * ls, SB_DIM), F32)
    vs_stack = jnp.zeros((DEPTH, bs * ls, SB_DIM), F32)
    sp, ss = [], []
    for l in range(DEPTH):
        lp = _mixer_params(l, short_conv_w, ssd_conv_w, ssd_conv_b, ssd_dt_bias, ssd_a_log, ssd_d, ssd_norm_g)
        xp, kp_stack, vp_stack, st_p = _layer(xp, bp, lp_, l, dw, lp, kp_stack, vp_stack, conv0, ssd_conv0, ssd0,
                                              PROMPT_CHUNK, PROMPT_Q_TILE, None, None)
        xs, ks_stack, vs_stack, st_s = _layer(
            xs, bs, ls, l, dw, lp, ks_stack, vs_stack, state_short_conv[l], state_ssd_conv[l], state_ssd[l], ls, ls,
            _channel_major(cache_sb_k[l]), _channel_major(cache_sb_v[l]))
        sp.append(st_p)
        ss.append(st_s)
    outs = [xp.reshape(bp, lp_, D_MODEL), xs.reshape(bs, ls, D_MODEL)]
    for (k_stack, v_stack, group, bsz, length) in ((kp_stack, vp_stack, sp, bp, lp_), (ks_stack, vs_stack, ss, bs, ls)):
        for stack in (k_stack, v_stack):
            if stack.ndim == 4:
                heads_first = stack.reshape(DEPTH, bsz, SB_HEADS, HEAD_DIM, length)
                outs.append(jnp.transpose(heads_first, (0, 1, 4, 2, 3)))
            else:
                outs.append(stack.reshape(DEPTH, bsz, length, SB_HEADS, HEAD_DIM))
        for j in range(3):
            outs.append(jnp.stack([s[j] for s in group]))
    return tuple(outs)
```
